```python
import math
import jax, jax.numpy as jnp
from jax import lax
import numpy as np

D_MODEL = 1024
BATCH = 8
SEQ = 4096
DEPTH = 4
DEC_BATCH = 8
DEC_SEQ = 16
PAST_LEN = 1024

CHUNK = 64
N_META = 16
QBLOCK = 128
EPS = 1e-6
H_A = 8
HD_A = 64
DA = H_A * 2 * HD_A
H_B = 8
Q_LORA = 768
KV_LORA = 256
NOPE_B = 64
ROPE_B = 32
V_B = 64
ROPE_THETA = 10000.0
NUM_BUCKETS = 32
MAX_DISTANCE = 128
D_FF = 2816
CONV_W = 3
IN_SPLITS = (DA, 2 * DA, 3 * DA, 3 * DA + Q_LORA, 3 * DA + Q_LORA + KV_LORA + ROPE_B,
             3 * DA + Q_LORA + KV_LORA + ROPE_B + D_MODEL)
N_IN = 3 * DA + Q_LORA + KV_LORA + ROPE_B + 2 * D_MODEL

kernel_name = "hybrid_diffattn_mla_convffn_stream_step"


def rms_norm(x, g):
    xf = x.astype(jnp.float32)
    y = xf * lax.rsqrt(jnp.mean(xf * xf, axis=-1, keepdims=True) + EPS)
    return (y * g.astype(jnp.float32)).astype(x.dtype)


def t5_bucket(rel):
    half = NUM_BUCKETS // 2
    max_exact = half // 2
    n = jnp.abs(rel)
    nf = jnp.maximum(n, 1).astype(jnp.float32)
    large = max_exact + (jnp.log(nf / max_exact) / math.log(MAX_DISTANCE / max_exact)
                         * (half - max_exact)).astype(jnp.int32)
    large = jnp.minimum(large, half - 1)
    return jnp.where(rel > 0, half, 0) + jnp.where(n < max_exact, n, large)


def rope(x, pos):
    half = ROPE_B // 2
    inv = ROPE_THETA ** (-jnp.arange(half, dtype=jnp.float32) / half)
    ang = pos.astype(jnp.float32)[:, None] * inv[None, :]
    shape = (1, pos.shape[0]) + (1,) * (x.ndim - 3) + (half,)
    c = jnp.cos(ang).reshape(shape)
    s = jnp.sin(ang).reshape(shape)
    xf = x.astype(jnp.float32)
    x1, x2 = xf[..., :half], xf[..., half:]
    return jnp.concatenate([x1 * c - x2 * s, x1 * s + x2 * c], axis=-1).astype(x.dtype)


def bcast(a, n):
    return jnp.broadcast_to(a, (n,) + a.shape[1:])


def over_query_blocks(fn, q_arrays, q_pos):
    lq = q_pos.shape[0]
    blk = min(QBLOCK, lq)
    nb = -(-lq // blk)
    pad = nb * blk - lq

    def split(a):
        a = jnp.pad(a, [(0, 0), (0, pad)] + [(0, 0)] * (a.ndim - 2))
        return jnp.moveaxis(a.reshape((a.shape[0], nb, blk) + a.shape[2:]), 1, 0)

    qs = tuple(split(a) for a in q_arrays)
    ps = jnp.pad(q_pos, (0, pad), mode="edge").reshape(nb, blk)
    out = lax.map(lambda args: fn(args[0], args[1]), (qs, ps))
    out = jnp.moveaxis(out, 0, 1)
    out = out.reshape((out.shape[0], nb * blk) + out.shape[3:])
    return out[:, :lq]


def diff_attention(q, k, v, q_pos, k_pos, lam, bias_table):
    scale = HD_A ** -0.5
    k_chunk = k_pos // CHUNK

    def block(qs, pos):
        (qb,) = qs
        s = jnp.einsum("bqhmd,bkhmd->bhmqk", qb, k).astype(jnp.float32) * scale
        bias = jnp.transpose(bias_table[t5_bucket(k_pos[None, :] - pos[:, None])], (2, 0, 1))
        s = s + bias.astype(jnp.float32)[None, :, None]
        mask = k_chunk[None, :] <= (pos // CHUNK)[:, None]
        p = jax.nn.softmax(jnp.where(mask, s, -jnp.inf), axis=-1)
        pd = p[:, :, 0] - lam * p[:, :, 1]
        return jnp.einsum("bhqk,bkhe->bqhe", pd, v).astype(v.dtype)

    return over_query_blocks(block, (q,), q_pos)


def mla_attention(q_nope, q_pe, k_nope, k_pe, v, q_pos, k_pos):
    scale = (NOPE_B + ROPE_B) ** -0.5
    k_chunk = k_pos // CHUNK

    def block(qs, pos):
        qn, qp = qs
        s = (jnp.einsum("bqhn,bkhn->bhqk", qn, k_nope)
             + jnp.einsum("bqhe,bke->bhqk", qp, k_pe)).astype(jnp.float32) * scale
        mask = k_chunk[None, :] <= (pos // CHUNK)[:, None]
        p = jax.nn.softmax(jnp.where(mask, s, -jnp.inf), axis=-1)
        return jnp.einsum("bhqk,bkhe->bqhe", p, v).astype(v.dtype)

    return over_query_blocks(block, (q_nope, q_pe), q_pos)


def trunk_layer(h, past_ak, past_av, past_ckv, past_kr, conv_buf, q_pos, k_pos, lam_init, bias_table, lw):
    (norm_mix_g, w_in, a_lambda, a_subln_g, b_q_norm_g, w_b_uq, b_kv_norm_g, w_b_ukv,
     w_a_br, w_b_br, w_out, norm_ffn_g, w_up, conv_w, conv_b, w_down) = lw
    b, L, _ = h.shape
    hn = rms_norm(h, norm_mix_g)
    z = jnp.einsum("bld,dn->bln", hn, w_in)
    qa, ka, va, cq, dkv, ga, gb = jnp.split(z, IN_SPLITS, axis=-1)

    qa = qa.reshape(b, L, H_A, 2, HD_A)
    ka = ka.reshape(b, L, H_A, 2 * HD_A)
    va = va.reshape(b, L, H_A, 2 * HD_A)
    k_all = jnp.concatenate([past_ak, ka], axis=1).reshape(b, -1, H_A, 2, HD_A)
    v_all = jnp.concatenate([past_av, va], axis=1)
    al = a_lambda.astype(jnp.float32)
    lam = jnp.exp(jnp.sum(al[0] * al[1])) - jnp.exp(jnp.sum(al[2] * al[3])) + lam_init
    oa = diff_attention(qa, k_all, v_all, q_pos, k_pos, lam, bias_table)
    oa = rms_norm(oa, a_subln_g) * (1.0 - lam_init)
    oa = jnp.einsum("blc,cd->bld", oa.reshape(b, L, DA), w_a_br)

    cq = rms_norm(cq, b_q_norm_g)
    qb = jnp.einsum("blr,rhe->blhe", cq, w_b_uq)
    q_nope = qb[..., :NOPE_B]
    q_pe = rope(qb[..., NOPE_B:], q_pos)
    ckv = rms_norm(dkv[..., :KV_LORA], b_kv_norm_g)
    kr = rope(dkv[..., KV_LORA:], q_pos)
    ckv_all = jnp.concatenate([past_ckv, ckv], axis=1)
    kr_all = jnp.concatenate([past_kr, kr], axis=1)
    kv_up = jnp.einsum("bkr,rhe->bkhe", ckv_all, w_b_ukv)
    ob = mla_attention(q_nope, q_pe, kv_up[..., :NOPE_B], kr_all, kv_up[..., NOPE_B:], q_pos, k_pos)
    ob = jnp.einsum("blc,cd->bld", ob.reshape(b, L, H_B * V_B), w_b_br)

    m = jax.nn.sigmoid(ga) * oa + jax.nn.sigmoid(gb) * ob
    h = h + jnp.einsum("bld,de->ble", m, w_out)

    hn2 = rms_norm(h, norm_ffn_g)
    u = jnp.einsum("bld,df->blf", hn2, w_up)
    ext = jnp.concatenate([conv_buf.astype(u.dtype), u], axis=1)
    c = conv_b + sum(conv_w[j] * ext[:, j:j + L] for j in range(CONV_W))
    a_up, g_up = jnp.split(c, 2, axis=-1)
    h = h + jnp.einsum("blf,fd->bld", jax.nn.silu(g_up) * a_up, w_down)
    return h, (ka, va, ckv, kr, ext[:, -(CONV_W - 1):])


def setup_inputs(seed: int = 0) -> dict:
    key = jax.random.key(seed)
    ks = jax.random.split(key, 28)
    f32 = jnp.float32

    def nrm(k, shape, scale=1.0):
        return jax.random.normal(k, shape, f32) * scale

    def gain(k, shape):
        return 1.0 + 0.05 * jax.random.normal(k, shape, f32)

    return {
        "x_prompt": nrm(ks[0], (BATCH, SEQ, D_MODEL)),
        "x_sample": nrm(ks[1], (DEC_BATCH, DEC_SEQ, D_MODEL)),
        "cache_a_k": nrm(ks[2], (DEPTH, DEC_BATCH, PAST_LEN, H_A, 2 * HD_A)),
        "cache_a_v": nrm(ks[3], (DEPTH, DEC_BATCH, PAST_LEN, H_A, 2 * HD_A)),
        "cache_b_ckv": nrm(ks[4], (DEPTH, DEC_BATCH, PAST_LEN, KV_LORA)),
        "cache_b_krope": nrm(ks[5], (DEPTH, DEC_BATCH, PAST_LEN, ROPE_B)),
        "state_ffn_conv": nrm(ks[6], (DEPTH, DEC_BATCH, CONV_W - 1, 2 * D_FF)),
        "meta_tokens": nrm(ks[7], (N_META, D_MODEL)),
        "rel_bias_table": nrm(ks[8], (NUM_BUCKETS, H_A), 0.5),
        "norm_mix_g": gain(ks[9], (DEPTH, D_MODEL)),
        "w_in": nrm(ks[10], (DEPTH, D_MODEL, N_IN), D_MODEL ** -0.5),
        "a_lambda": nrm(ks[11], (DEPTH, 4, HD_A), 0.1),
        "a_subln_g": gain(ks[12], (DEPTH, 2 * HD_A)),
        "b_q_norm_g": gain(ks[13], (DEPTH, Q_LORA)),
        "w_b_uq": nrm(ks[14], (DEPTH, Q_LORA, H_B, NOPE_B + ROPE_B), Q_LORA ** -0.5),
        "b_kv_norm_g": gain(ks[15], (DEPTH, KV_LORA)),
        "w_b_ukv": nrm(ks[16], (DEPTH, KV_LORA, H_B, NOPE_B + V_B), KV_LORA ** -0.5),
        "w_a_br": nrm(ks[17], (DEPTH, DA, D_MODEL), DA ** -0.5),
        "w_b_br": nrm(ks[18], (DEPTH, H_B * V_B, D_MODEL), (H_B * V_B) ** -0.5),
        "w_out": nrm(ks[19], (DEPTH, D_MODEL, D_MODEL), D_MODEL ** -0.5),
        "norm_ffn_g": gain(ks[20], (DEPTH, D_MODEL)),
        "w_up": nrm(ks[21], (DEPTH, D_MODEL, 2 * D_FF), D_MODEL ** -0.5),
        "conv_w": nrm(ks[22], (DEPTH, CONV_W, 2 * D_FF), CONV_W ** -0.5),
        "conv_b": nrm(ks[23], (DEPTH, 2 * D_FF), 0.02),
        "w_down": nrm(ks[24], (DEPTH, D_FF, D_MODEL), D_FF ** -0.5),
        "final_norm_g": gain(ks[25], (D_MODEL,)),
    }


def reference(x_prompt, x_sample, cache_a_k, cache_a_v, cache_b_ckv, cache_b_krope, state_ffn_conv,
              meta_tokens, rel_bias_table, norm_mix_g, w_in, a_lambda, a_subln_g, b_q_norm_g, w_b_uq,
              b_kv_norm_g, w_b_ukv, w_a_br, w_b_br, w_out, norm_ffn_g, w_up, conv_w, conv_b, w_down,
              final_norm_g):
    bp, seq = x_prompt.shape[0], x_prompt.shape[1]
    bs, dec = x_sample.shape[0], x_sample.shape[1]
    past_len = cache_a_k.shape[2]
    dt = x_prompt.dtype
    meta_pos = jnp.arange(-N_META, 0, dtype=jnp.int32)
    prompt_pos = jnp.arange(seq, dtype=jnp.int32)
    sample_pos = jnp.arange(past_len, past_len + dec, dtype=jnp.int32)
    kpos_prompt = jnp.concatenate([meta_pos, prompt_pos])
    kpos_sample = jnp.concatenate([meta_pos, jnp.arange(past_len, dtype=jnp.int32), sample_pos])

    h_meta = meta_tokens[None].astype(dt)
    h_p = x_prompt
    h_s = x_sample
    ak_p, ak_s, av_p, av_s, ck_p, ck_s, kr_p, kr_s, cv_p, cv_s = ([] for _ in range(10))
    for l in range(DEPTH):
        lam_init = 0.8 - 0.6 * math.exp(-0.3 * l)
        lw = (norm_mix_g[l], w_in[l], a_lambda[l], a_subln_g[l], b_q_norm_g[l], w_b_uq[l], b_kv_norm_g[l],
              w_b_ukv[l], w_a_br[l], w_b_br[l], w_out[l], norm_ffn_g[l], w_up[l], conv_w[l], conv_b[l],
              w_down[l])
        h_meta, (mk, mv, mc, mr, mbuf) = trunk_layer(
            h_meta,
            jnp.zeros((1, 0, H_A, 2 * HD_A), dt), jnp.zeros((1, 0, H_A, 2 * HD_A), dt),
            jnp.zeros((1, 0, KV_LORA), dt), jnp.zeros((1, 0, ROPE_B), dt),
            jnp.zeros((1, CONV_W - 1, 2 * D_FF), dt),
            meta_pos, meta_pos, lam_init, rel_bias_table, lw)
        h_p, (pk, pv, pc, pr, pbuf) = trunk_layer(
            h_p, bcast(mk, bp), bcast(mv, bp), bcast(mc, bp), bcast(mr, bp), bcast(mbuf, bp),
            prompt_pos, kpos_prompt, lam_init, rel_bias_table, lw)
        h_s, (sk, sv, sc, sr, sbuf) = trunk_layer(
            h_s,
            jnp.concatenate([bcast(mk, bs), cache_a_k[l]], axis=1),
            jnp.concatenate([bcast(mv, bs), cache_a_v[l]], axis=1),
            jnp.concatenate([bcast(mc, bs), cache_b_ckv[l]], axis=1),
            jnp.concatenate([bcast(mr, bs), cache_b_krope[l]], axis=1),
            state_ffn_conv[l],
            sample_pos, kpos_sample, lam_init, rel_bias_table, lw)
        ak_p.append(jnp.concatenate([bcast(mk, bp), pk], axis=1))
        av_p.append(jnp.concatenate([bcast(mv, bp), pv], axis=1))
        ck_p.append(jnp.concatenate([bcast(mc, bp), pc], axis=1))
        kr_p.append(jnp.concatenate([bcast(mr, bp), pr], axis=1))
        cv_p.append(pbuf)
        ak_s.append(sk)
        av_s.append(sv)
        ck_s.append(sc)
        kr_s.append(sr)
        cv_s.append(sbuf)

    y_prompt = rms_norm(h_p, final_norm_g)
    y_sample = rms_norm(h_s, final_norm_g)
    return (y_prompt, y_sample,
            jnp.stack(ak_p), jnp.stack(ak_s), jnp.stack(av_p), jnp.stack(av_s),
            jnp.stack(ck_p), jnp.stack(ck_s), jnp.stack(kr_p), jnp.stack(kr_s),
            jnp.stack(cv_p), jnp.stack(cv_s))
```

```python
import functools
import math

import numpy as np
import jax
import jax.numpy as jnp
from jax import lax
from jax.experimental import pallas as pl
from jax.experimental.pallas import tpu as pltpu

CHUNK = 64
N_META = 16
EPS = 1e-6
H_A = 8
HD_A = 64
H_B = 8
NOPE_B = 64
ROPE_B = 32
V_B = 64
ROPE_THETA = 10000.0
NUM_BUCKETS = 32
MAX_DISTANCE = 128
CONV_W = 3

LANES = 128
SUBLANES = 8
LOG2E = 1.4426950408889634
MASKED = -1e30
FAR_REL = 91
VMEM_LIMIT = 56 * 1024 * 1024

F32 = jnp.float32
BF16 = jnp.bfloat16
HB_PAD = LANES
ROPE_LO = NOPE_B
ROPE_HALF = ROPE_B // 2


def _dot(a, b):
    return jnp.dot(a, b, preferred_element_type=F32)


def _dot_nt(a, b):
    return lax.dot_general(a, b, (((1,), (1,)), ((), ())), preferred_element_type=F32)


def _rms(x, g):
    return x * lax.rsqrt(jnp.mean(x * x, axis=-1, keepdims=True) + EPS) * g


def _rope_padded(x, cos, sin_lo, sin_hi):
    return (x * cos + pltpu.roll(x, LANES - ROPE_HALF, 1) * sin_lo
            + pltpu.roll(x, ROPE_HALF, 1) * sin_hi)


def _const_spec(shape):
    nd = len(shape)
    return pl.BlockSpec(shape, lambda *_: (0,) * nd, pipeline_mode=pl.Buffered(1))


def _proj_kernel(x_ref, g_ref, wq_ref, wk_ref, wv_ref, wv2_ref, wcq_ref, wckv_ref, wkr_ref, qng_ref,
                 wuq_ref, kvng_ref, wuk_ref, wuv_ref, cos_ref, slo_ref, shi_ref,
                 qa1_ref, qa2_ref, kf_ref, k16_ref, vf_ref, v16_ref, qb_ref, ckv_ref, kr_ref, kb_ref,
                 vb_ref, *, transposed_v):
    x = x_ref[...]
    hb = _rms(x, g_ref[...]).astype(BF16)

    qa = _dot(hb, wq_ref[...]) * (HD_A ** -0.5 * LOG2E)
    lane = lax.broadcasted_iota(jnp.int32, qa.shape, 1) % (2 * HD_A)
    qa1_ref[...] = jnp.where(lane < HD_A, qa, 0.0).astype(BF16)
    qa2_ref[...] = jnp.where(lane >= HD_A, qa, 0.0).astype(BF16)

    k = _dot(hb, wk_ref[...])
    kf_ref[...] = k
    k16_ref[...] = k.astype(BF16)
    v = _dot(hb, wv_ref[...])
    vf_ref[...] = v
    if transposed_v:
        v16_ref[...] = _dot_nt(wv2_ref[...], hb).astype(BF16)
    else:
        v16_ref[...] = v.astype(BF16)

    cos = cos_ref[...]
    slo = slo_ref[...]
    shi = shi_ref[...]

    cq = _rms(_dot(hb, wcq_ref[...]), qng_ref[...]).astype(BF16)
    qb = _dot(cq, wuq_ref[...]) * ((NOPE_B + ROPE_B) ** -0.5 * LOG2E)
    for h in range(H_B):
        sl = slice(h * HB_PAD, (h + 1) * HB_PAD)
        qb_ref[:, sl] = _rope_padded(qb[:, sl], cos, slo, shi).astype(BF16)

    ckv = _rms(_dot(hb, wckv_ref[...]), kvng_ref[...])
    ckv_ref[...] = ckv
    kr = _rope_padded(_dot(hb, wkr_ref[...]), cos, slo, shi)
    kr_ref[...] = kr[:, ROPE_LO:ROPE_LO + ROPE_B]
    cb = ckv.astype(BF16)
    kn = _dot(cb, wuk_ref[...])
    for h in range(H_B):
        sl = slice(h * HB_PAD, (h + 1) * HB_PAD)
        kb_ref[:, sl] = (kn[:, sl] + kr).astype(BF16)
    if transposed_v:
        vb_ref[...] = _dot_nt(wuv_ref[...], cb).astype(BF16)
    else:
        vb_ref[...] = _dot(cb, wuv_ref[...]).astype(BF16)


def _projection(x, lw, tabs, tm, transposed_v):
    b, L, d = x.shape
    da = lw["wq"].shape[1]
    kvl = lw["wckv"].shape[1]
    hv = H_B * V_B
    nj = L // tm
    row = lambda w: pl.BlockSpec((None, tm, w), lambda i, j: (i, j, 0))
    col = lambda w: pl.BlockSpec((None, w, tm), lambda i, j: (i, 0, j))
    tab = pl.BlockSpec((tm, LANES), lambda i, j: (j, 0))
    wuv = lw["wuv_t"] if transposed_v else lw["wuv"]
    weights = [lw["g_mix"], lw["wq"], lw["wk"], lw["wv"], lw["wv_t"], lw["wcq"], lw["wckv"], lw["wkr"],
               lw["g_q"], lw["wuq"], lw["g_kv"], lw["wuk"], wuv]
    in_specs = [row(d)] + [_const_spec(w.shape) for w in weights] + [tab, tab, tab]
    sds = jax.ShapeDtypeStruct
    if transposed_v:
        v16_shape, v16_spec = sds((b, da, L), BF16), col(da)
        vb_shape, vb_spec = sds((b, hv, L), BF16), col(hv)
    else:
        v16_shape, v16_spec = sds((b, L, da), BF16), row(da)
        vb_shape, vb_spec = sds((b, L, hv), BF16), row(hv)
    out_shape = [sds((b, L, da), BF16), sds((b, L, da), BF16), sds((b, L, da), F32), sds((b, L, da), BF16),
                 sds((b, L, da), F32), v16_shape, sds((b, L, H_B * HB_PAD), BF16), sds((b, L, kvl), F32),
                 sds((b, L, ROPE_B), F32), sds((b, L, H_B * HB_PAD), BF16), vb_shape]
    out_specs = [row(da), row(da), row(da), row(da), row(da), v16_spec, row(H_B * HB_PAD), row(kvl),
                 row(ROPE_B), row(H_B * HB_PAD), vb_spec]
    outs = pl.pallas_call(
        functools.partial(_proj_kernel, transposed_v=transposed_v),
        grid=(b, nj), in_specs=in_specs, out_specs=out_specs, out_shape=out_shape,
        compiler_params=pltpu.CompilerParams(dimension_semantics=("parallel", "parallel"),
                                             vmem_limit_bytes=VMEM_LIMIT),
    )(x, *weights, *tabs)
    names = ("qa1", "qa2", "kf", "k16", "vf", "v16", "qb", "ckv", "kr", "kb", "vb")
    return dict(zip(names, outs))


def _kvup_kernel(ckv_ref, kr_ref, wuk_ref, wuvt_ref, kb_ref, vbt_ref):
    cb = ckv_ref[...].astype(BF16)
    kn = _dot(cb, wuk_ref[...])
    kr = kr_ref[...]
    for h in range(H_B):
        sl = slice(h * HB_PAD, (h + 1) * HB_PAD)
        kb_ref[:, sl] = (kn[:, sl] + kr).astype(BF16)
    vbt_ref[...] = _dot_nt(wuvt_ref[...], cb).astype(BF16)


def _kv_up_cache(ckv, kr_pad, lw, tm):
    b, p, kvl = ckv.shape
    hv = H_B * V_B
    row = lambda w: pl.BlockSpec((None, tm, w), lambda i, j: (i, j, 0))
    return pl.pallas_call(
        _kvup_kernel, grid=(b, p // tm),
        in_specs=[row(kvl), row(LANES), _const_spec(lw["wuk"].shape), _const_spec(lw["wuv_t"].shape)],
        out_specs=[row(H_B * HB_PAD), pl.BlockSpec((None, hv, tm), lambda i, j: (i, 0, j))],
        out_shape=[jax.ShapeDtypeStruct((b, p, H_B * HB_PAD), BF16), jax.ShapeDtypeStruct((b, hv, p), BF16)],
        compiler_params=pltpu.CompilerParams(dimension_semantics=("parallel", "parallel"),
                                             vmem_limit_bytes=VMEM_LIMIT),
    )(ckv, kr_pad, lw["wuk"], lw["wuv_t"])


def _online_softmax_step(st, vt, m_ref, l_ref, acc_ref, idx):
    row = slice(idx, idx + 1)
    m_old = m_ref[row, :]
    m_new = jnp.maximum(m_old, jnp.max(st, axis=0, keepdims=True))
    alpha = jnp.exp2(m_old - m_new)
    p = jnp.exp2(st - m_new)
    l_ref[row, :] = alpha * l_ref[row, :] + jnp.sum(p, axis=0, keepdims=True)
    m_ref[row, :] = m_new
    acc_ref[idx] = alpha * acc_ref[idx] + _dot(vt, p.astype(BF16))


def _attn_kernel(qi_ref, ki_ref, var_ref, mvar_ref, flag_ref, *refs, has_meta, lam_init):
    del qi_ref, ki_ref, var_ref, mvar_ref
    qa1_ref, qa2_ref, ka_ref, vat_ref, qb_ref, kb_ref, vbt_ref, bias_ref = refs[:8]
    refs = refs[8:]
    if has_meta:
        mka_ref, mvat_ref, mkb_ref, mvbt_ref, mbias_ref = refs[:5]
        refs = refs[5:]
    lam_ref, sg_ref, oa_ref, ob_ref, ma, la, acca, mb, lb, accb = refs

    flags = flag_ref[pl.program_id(1)]
    first = (flags & 1) != 0
    last = (flags & 2) != 0

    def process(k_ref, vt_ref, k2_ref, vt2_ref, b_ref):
        for h in range(H_A):
            sl = slice(h * 2 * HD_A, (h + 1) * 2 * HD_A)
            kh = k_ref[:, sl]
            vth = vt_ref[sl, :]
            for mp, q_ref in enumerate((qa1_ref, qa2_ref)):
                st = _dot_nt(kh, q_ref[:, sl]) + b_ref[h]
                _online_softmax_step(st, vth, ma, la, acca, 2 * h + mp)
        for h in range(H_B):
            sl = slice(h * HB_PAD, (h + 1) * HB_PAD)
            st = _dot_nt(k2_ref[:, sl], qb_ref[:, sl]) + b_ref[H_A]
            _online_softmax_step(st, vt2_ref[h * V_B:(h + 1) * V_B, :], mb, lb, accb, h)

    @pl.when(first)
    def _():
        ma[...] = jnp.full(ma.shape, MASKED, F32)
        mb[...] = jnp.full(mb.shape, MASKED, F32)
        la[...] = jnp.zeros(la.shape, F32)
        lb[...] = jnp.zeros(lb.shape, F32)
        acca[...] = jnp.zeros(acca.shape, F32)
        accb[...] = jnp.zeros(accb.shape, F32)
        if has_meta:
            process(mka_ref, mvat_ref, mkb_ref, mvbt_ref, mbias_ref)

    process(ka_ref, vat_ref, kb_ref, vbt_ref, bias_ref)

    @pl.when(last)
    def _():
        al = lam_ref[...]
        lam = (jnp.exp(jnp.sum(al[0:1] * al[1:2], axis=1, keepdims=True))
               - jnp.exp(jnp.sum(al[2:3] * al[3:4], axis=1, keepdims=True)) + lam_init)
        sg = sg_ref[...]
        for h in range(H_A):
            o = acca[2 * h] / la[2 * h:2 * h + 1, :] - lam * (acca[2 * h + 1] / la[2 * h + 1:2 * h + 2, :])
            y = o * lax.rsqrt(jnp.mean(o * o, axis=0, keepdims=True) + EPS) * sg * (1.0 - lam_init)
            oa_ref[:, h * 2 * HD_A:(h + 1) * 2 * HD_A] = y.T.astype(BF16)
        for j in range(H_B // 2):
            pair = jnp.concatenate([accb[2 * j] / lb[2 * j:2 * j + 1, :],
                                    accb[2 * j + 1] / lb[2 * j + 1:2 * j + 2, :]], axis=0)
            ob_ref[:, j * 2 * V_B:(j + 1) * 2 * V_B] = pair.T.astype(BF16)


def _t5_bucket(rel):
    half = NUM_BUCKETS // 2
    max_exact = half // 2
    n = jnp.abs(rel)
    nf = jnp.maximum(n, 1).astype(F32)
    large = max_exact + (jnp.log(nf / max_exact) / math.log(MAX_DISTANCE / max_exact)
                         * (half - max_exact)).astype(jnp.int32)
    large = jnp.minimum(large, half - 1)
    return jnp.where(rel > 0, half, 0) + jnp.where(n < max_exact, n, large)


def _bias_tile(table, q_pos, k_pos, k_valid):
    q_pos = jnp.asarray(q_pos, jnp.int32)
    k_pos = jnp.asarray(k_pos, jnp.int32)
    rel = k_pos[:, None] - q_pos[None, :]
    bias = jnp.transpose(table.astype(F32)[_t5_bucket(rel)], (2, 0, 1)) * LOG2E
    bias = jnp.concatenate([bias, jnp.zeros((1,) + rel.shape, F32)], axis=0)
    visible = ((k_pos[:, None] // CHUNK) <= (q_pos[None, :] // CHUNK)) & jnp.asarray(k_valid)[:, None]
    return jnp.where(visible[None], bias, MASKED)


def _plan_tiles(q_pos, k_pos, k_valid, tq, tk, all_pairs):
    nq, nk = len(q_pos) // tq, len(k_pos) // tk
    pairs, variants, defs = [], {}, []
    for qi in range(nq):
        qp = q_pos[qi * tq:(qi + 1) * tq]
        for ki in range(nk):
            kp = k_pos[ki * tk:(ki + 1) * tk]
            kv = k_valid[ki * tk:(ki + 1) * tk]
            vis = ((kp[:, None] // CHUNK) <= (qp[None, :] // CHUNK)) & kv[:, None]
            if not (all_pairs or vis.any()):
                continue
            rel = kp[:, None] - qp[None, :]
            if vis.all() and rel.max() <= -FAR_REL:
                key = ("far",)
            else:
                key = (int(kp[0] - qp[0]), int(qp[0] % CHUNK), int(kp[0] % CHUNK), int(kv.sum()),
                       tuple(np.diff(qp).tolist()) if np.any(np.diff(qp) != 1) else ())
            if key not in variants:
                variants[key] = len(defs)
                defs.append((qp, kp, kv))
            pairs.append((qi, ki, variants[key]))
    return pairs, defs


def _attention(p, meta, table, a_lambda, subln_g, lam_init, q_pos, k_pos, k_valid, tq, tk):
    b, lq, da = p["qa1"].shape
    hv = H_B * V_B
    pairs, defs = _plan_tiles(q_pos, k_pos, k_valid, tq, tk, all_pairs=False)
    bias = jnp.stack([_bias_tile(table, *d) for d in defs])
    nq = lq // tq
    qi = np.array([a for a, _, _ in pairs], np.int32)
    ki = np.array([a for _, a, _ in pairs], np.int32)
    var = np.array([a for _, _, a in pairs], np.int32)
    flags = np.zeros(len(pairs), np.int32)
    for s in range(len(pairs)):
        if s == 0 or qi[s] != qi[s - 1]:
            flags[s] |= 1
        if s == len(pairs) - 1 or qi[s] != qi[s + 1]:
            flags[s] |= 2
    assert sorted(set(qi.tolist())) == list(range(nq))
    has_meta = meta is not None
    if has_meta:
        m_pos = np.concatenate([np.arange(-N_META, 0), np.zeros(LANES - N_META, np.int64)])
        m_valid = np.arange(LANES) < N_META
        mpairs, mdefs = _plan_tiles(q_pos, m_pos, m_valid, tq, LANES, all_pairs=True)
        mbias = jnp.stack([_bias_tile(table, *d) for d in mdefs])
        mvar_q = np.array([a for _, _, a in mpairs], np.int32)
        mvar = mvar_q[qi]
    else:
        mvar = np.zeros(len(pairs), np.int32)

    qblk = lambda w: pl.BlockSpec((None, tq, w), lambda i, s, qi_, ki_, v_, mv_, f_: (i, qi_[s], 0))
    kblk = lambda w: pl.BlockSpec((None, tk, w), lambda i, s, qi_, ki_, v_, mv_, f_: (i, ki_[s], 0))
    ktblk = lambda w: pl.BlockSpec((None, w, tk), lambda i, s, qi_, ki_, v_, mv_, f_: (i, 0, ki_[s]))
    const = lambda shape: pl.BlockSpec(shape, lambda *_: (0,) * len(shape))
    in_specs = [qblk(da), qblk(da), kblk(da), ktblk(da), qblk(H_B * HB_PAD), kblk(H_B * HB_PAD), ktblk(hv),
                pl.BlockSpec((None, H_A + 1, tk, tq), lambda i, s, qi_, ki_, v_, mv_, f_: (v_[s], 0, 0, 0))]
    args = [p["qa1"], p["qa2"], p["ka"], p["vat"], p["qb"], p["kb"], p["vbt"], bias]
    if has_meta:
        in_specs += [const(meta["ka"].shape), const(meta["vat"].shape), const(meta["kb"].shape),
                     const(meta["vbt"].shape),
                     pl.BlockSpec((None, H_A + 1, LANES, tq),
                                  lambda i, s, qi_, ki_, v_, mv_, f_: (mv_[s], 0, 0, 0))]
        args += [meta["ka"], meta["vat"], meta["kb"], meta["vbt"], mbias]
    in_specs += [const(a_lambda.shape), const((2 * HD_A, 1))]
    args += [a_lambda, subln_g.reshape(2 * HD_A, 1)]
    grid_spec = pltpu.PrefetchScalarGridSpec(
        num_scalar_prefetch=5, grid=(b, len(pairs)), in_specs=in_specs,
        out_specs=[qblk(da), qblk(hv)],
        scratch_shapes=[pltpu.VMEM((2 * H_A, tq), F32), pltpu.VMEM((2 * H_A, tq), F32),
                        pltpu.VMEM((2 * H_A, 2 * HD_A, tq), F32),
                        pltpu.VMEM((H_B, tq), F32), pltpu.VMEM((H_B, tq), F32),
                        pltpu.VMEM((H_B, V_B, tq), F32)])
    oa, ob = pl.pallas_call(
        functools.partial(_attn_kernel, has_meta=has_meta, lam_init=lam_init),
        grid_spec=grid_spec,
        out_shape=[jax.ShapeDtypeStruct((b, lq, da), BF16), jax.ShapeDtypeStruct((b, lq, hv), BF16)],
        compiler_params=pltpu.CompilerParams(dimension_semantics=("parallel", "arbitrary"),
                                             vmem_limit_bytes=VMEM_LIMIT),
    )(jnp.asarray(qi), jnp.asarray(ki), jnp.asarray(var), jnp.asarray(mvar), jnp.asarray(flags), *args)
    return oa, ob


def _post_kernel(x_ref, oa_ref, ob_ref, gmix_ref, wg_ref, wabr_ref, wbbr_ref, wout_ref, gffn_ref, wup_ref,
                 cw_ref, cb_ref, wdn_ref, cbuf_ref, h_ref, cst_ref, acc_ref, ext_ref, carry_ref,
                 *, tm, fc, n_chunks):
    j = pl.program_id(1)
    d = x_ref.shape[-1]
    x = x_ref[...]
    hb = _rms(x, gmix_ref[...]).astype(BF16)
    gates = jax.nn.sigmoid(_dot(hb, wg_ref[...]))
    m = gates[:, :d] * _dot(oa_ref[...], wabr_ref[...]) + gates[:, d:] * _dot(ob_ref[...], wbbr_ref[...])
    h = x + _dot(m.astype(BF16), wout_ref[...])
    hb2 = _rms(h, gffn_ref[...]).astype(BF16)

    @pl.when(j == 0)
    def _():
        carry_ref[...] = cbuf_ref[...]

    acc_ref[...] = h
    for c in range(n_chunks):
        cols = slice(c * 2 * fc, (c + 1) * 2 * fc)
        u = _dot(hb2, wup_ref[:, cols])
        ext_ref[0:SUBLANES, :] = carry_ref[:, cols]
        ext_ref[SUBLANES:SUBLANES + tm, :] = u
        cw = cw_ref[:, cols]
        conv = (cb_ref[:, cols] + cw[0:1] * ext_ref[SUBLANES - 2:SUBLANES - 2 + tm, :]
                + cw[1:2] * ext_ref[SUBLANES - 1:SUBLANES - 1 + tm, :] + cw[2:3] * u)
        carry_ref[:, cols] = ext_ref[tm:tm + SUBLANES, :]
        a_up = conv[:, :fc]
        g_up = conv[:, fc:]
        act = (g_up * jax.nn.sigmoid(g_up) * a_up).astype(BF16)
        acc_ref[...] += _dot(act, wdn_ref[c * fc:(c + 1) * fc, :])
    h_ref[...] = acc_ref[...]

    @pl.when(j == pl.num_programs(1) - 1)
    def _():
        cst_ref[...] = carry_ref[...]


def _post(x, oa, ob, cbuf8, lw, tm):
    b, L, d = x.shape
    ff2 = lw["wup"].shape[1]
    fc = lw["fc"]
    n_chunks = ff2 // (2 * fc)
    row = lambda w: pl.BlockSpec((None, tm, w), lambda i, j: (i, j, 0))
    st = pl.BlockSpec((None, SUBLANES, ff2), lambda i, j: (i, 0, 0))
    weights = [lw["g_mix"], lw["wg"], lw["wabr"], lw["wbbr"], lw["wout"], lw["g_ffn"], lw["wup"],
               lw["conv_w"], lw["conv_b"], lw["wdn"]]
    return pl.pallas_call(
        functools.partial(_post_kernel, tm=tm, fc=fc, n_chunks=n_chunks),
        grid=(b, L // tm),
        in_specs=[row(d), row(oa.shape[-1]), row(ob.shape[-1])] + [_const_spec(w.shape) for w in weights] + [st],
        out_specs=[row(d), st],
        out_shape=[jax.ShapeDtypeStruct((b, L, d), F32), jax.ShapeDtypeStruct((b, SUBLANES, ff2), F32)],
        scratch_shapes=[pltpu.VMEM((tm, d), F32), pltpu.VMEM((tm + SUBLANES, 2 * fc), F32),
                        pltpu.VMEM((SUBLANES, ff2), F32)],
        compiler_params=pltpu.CompilerParams(dimension_semantics=("parallel", "arbitrary"),
                                             vmem_limit_bytes=VMEM_LIMIT),
    )(x, oa, ob, *weights, cbuf8)


def _norm_kernel(x_ref, g_ref, o_ref):
    o_ref[...] = _rms(x_ref[...], g_ref[...])


def _final_norm(x, g, tm):
    b, L, d = x.shape
    row = pl.BlockSpec((None, tm, d), lambda i, j: (i, j, 0))
    return pl.pallas_call(
        _norm_kernel, grid=(b, L // tm), in_specs=[row, _const_spec((1, d))], out_specs=row,
        out_shape=jax.ShapeDtypeStruct((b, L, d), F32),
        compiler_params=pltpu.CompilerParams(dimension_semantics=("parallel", "parallel")),
    )(x, g.reshape(1, d))


def _layer_weights(l, norm_mix_g, w_in, b_q_norm_g, w_b_uq, b_kv_norm_g, w_b_ukv, w_a_br, w_b_br, w_out,
                   norm_ffn_g, w_up, conv_w, conv_b, w_down):
    d = w_in.shape[1]
    da = H_A * 2 * HD_A
    q_lora = w_b_uq.shape[1]
    kvl = w_b_ukv.shape[1]
    d_ff = w_down.shape[1]
    w = w_in[l]
    o_cq = 3 * da
    o_ckv = o_cq + q_lora
    o_kr = o_ckv + kvl
    o_g = o_kr + ROPE_B
    bf = lambda a: a.astype(BF16)
    wv = w[:, 2 * da:3 * da]
    wkr = jnp.zeros((d, LANES), F32).at[:, ROPE_LO:ROPE_LO + ROPE_B].set(w[:, o_kr:o_g])
    wuq = jnp.pad(w_b_uq[l], ((0, 0), (0, 0), (0, HB_PAD - NOPE_B - ROPE_B))).reshape(q_lora, H_B * HB_PAD)
    wuk = jnp.pad(w_b_ukv[l][:, :, :NOPE_B], ((0, 0), (0, 0), (0, HB_PAD - NOPE_B))).reshape(kvl, H_B * HB_PAD)
    wuv = w_b_ukv[l][:, :, NOPE_B:].reshape(kvl, H_B * V_B)
    fc = 256 if d_ff % 256 == 0 else d_ff
    nch = d_ff // fc
    regroup = lambda a: a.reshape(a.shape[0], 2, nch, fc).transpose(0, 2, 1, 3).reshape(a.shape[0], 2 * d_ff)
    return dict(
        g_mix=norm_mix_g[l].reshape(1, d), wq=bf(w[:, :da]), wk=bf(w[:, da:2 * da]), wv=bf(wv), wv_t=bf(wv.T),
        wcq=bf(w[:, o_cq:o_ckv]), wckv=bf(w[:, o_ckv:o_kr]), wkr=bf(wkr), wg=bf(w[:, o_g:]),
        g_q=b_q_norm_g[l].reshape(1, q_lora), wuq=bf(wuq), g_kv=b_kv_norm_g[l].reshape(1, kvl), wuk=bf(wuk),
        wuv=bf(wuv), wuv_t=bf(wuv.T), wabr=bf(w_a_br[l]), wbbr=bf(w_b_br[l]), wout=bf(w_out[l]),
        g_ffn=norm_ffn_g[l].reshape(1, d), wup=bf(regroup(w_up[l])),
        conv_w=jnp.pad(regroup(conv_w[l]), ((0, SUBLANES - CONV_W), (0, 0))),
        conv_b=regroup(conv_b[l].reshape(1, 2 * d_ff)), wdn=bf(w_down[l]), fc=fc, nch=nch, d_ff=d_ff)


def _ungroup_conv_state(cst, lw):
    b = cst.shape[0]
    s = cst[:, SUBLANES - (CONV_W - 1):, :].reshape(b, CONV_W - 1, lw["nch"], 2, lw["fc"])
    return s.transpose(0, 1, 3, 2, 4).reshape(b, CONV_W - 1, 2 * lw["d_ff"])


def _group_conv_state(buf, lw):
    b = buf.shape[0]
    s = buf.reshape(b, CONV_W - 1, 2, lw["nch"], lw["fc"]).transpose(0, 1, 3, 2, 4)
    s = s.reshape(b, CONV_W - 1, 2 * lw["d_ff"])
    return jnp.pad(s, ((0, 0), (SUBLANES - (CONV_W - 1), 0), (0, 0)))


def _rope_tables(pos):
    inv = ROPE_THETA ** (-jnp.arange(ROPE_HALF, dtype=F32) / ROPE_HALF)
    ang = jnp.asarray(pos).astype(F32)[:, None] * inv[None, :]
    c, s = jnp.cos(ang), jnp.sin(ang)
    z = jnp.zeros_like(c)
    n = pos.shape[0]
    tail = jnp.zeros((n, LANES - ROPE_LO - ROPE_B), F32)
    head0 = jnp.zeros((n, ROPE_LO), F32)
    cos = jnp.concatenate([jnp.ones((n, ROPE_LO), F32), c, c, tail], axis=1)
    sin_lo = jnp.concatenate([head0, -s, z, tail], axis=1)
    sin_hi = jnp.concatenate([head0, z, s, tail], axis=1)
    return cos, sin_lo, sin_hi


def _pad_rows(a, n, axis):
    pad = [(0, 0)] * a.ndim
    pad[axis] = (0, n - a.shape[axis])
    return jnp.pad(a, pad)


def _row_tile(L, want):
    t = min(L, want)
    while L % t:
        t //= 2
    return t


def kernel(x_prompt, x_sample, cache_a_k, cache_a_v, cache_b_ckv, cache_b_krope, state_ffn_conv, meta_tokens,
           rel_bias_table, norm_mix_g, w_in, a_lambda, a_subln_g, b_q_norm_g, w_b_uq, b_kv_norm_g, w_b_ukv,
           w_a_br, w_b_br, w_out, norm_ffn_g, w_up, conv_w, conv_b, w_down, final_norm_g):
    bp, seq, d = x_prompt.shape
    bs, dec, _ = x_sample.shape
    depth = w_in.shape[0]
    past = cache_a_k.shape[2]
    da = H_A * 2 * HD_A
    ff2 = w_up.shape[2]
    assert seq % CHUNK == 0 and N_META % SUBLANES == 0 and dec % SUBLANES == 0

    meta_pos = np.arange(-N_META, 0)
    prompt_pos = np.arange(seq)
    sample_pos = np.arange(past, past + dec)
    tabs_m, tabs_p, tabs_s = _rope_tables(meta_pos), _rope_tables(prompt_pos), _rope_tables(sample_pos)

    tm_p = _row_tile(seq, 256)
    tq_p = tk_p = _row_tile(seq, 256)
    qpad = lambda a: _pad_rows(a, LANES, 1)
    mq_pos = np.concatenate([meta_pos, np.full(LANES - N_META, -1)])
    mk_pos = mq_pos
    mk_valid = np.arange(LANES) < N_META
    sq_pos = np.concatenate([sample_pos, np.full(LANES - dec, sample_pos[-1])])
    lk_s = -(-(past + dec) // LANES) * LANES
    sk_pos = np.concatenate([np.arange(past + dec), np.zeros(lk_s - past - dec, np.int64)])
    sk_valid = np.arange(lk_s) < past + dec
    tm_c = _row_tile(past, 256)

    h_m = meta_tokens[None].astype(F32)
    h_p, h_s = x_prompt, x_sample
    outs = {k: [] for k in ("ak_p", "ak_s", "av_p", "av_s", "ck_p", "ck_s", "kr_p", "kr_s", "cv_p", "cv_s")}
    for l in range(depth):
        lam_init = 0.8 - 0.6 * math.exp(-0.3 * l)
        lw = _layer_weights(l, norm_mix_g, w_in, b_q_norm_g, w_b_uq, b_kv_norm_g, w_b_ukv, w_a_br, w_b_br,
                            w_out, norm_ffn_g, w_up, conv_w, conv_b, w_down)
        attn = functools.partial(_attention, table=rel_bias_table, a_lambda=a_lambda[l],
                                 subln_g=a_subln_g[l], lam_init=lam_init)

        pm = _projection(h_m, lw, tabs_m, N_META, transposed_v=False)
        meta = dict(ka=_pad_rows(pm["k16"][0], LANES, 0), vat=_pad_rows(pm["v16"][0].T, LANES, 1),
                    kb=_pad_rows(pm["kb"][0], LANES, 0), vbt=_pad_rows(pm["vb"][0].T, LANES, 1))
        qm = dict(qa1=qpad(pm["qa1"]), qa2=qpad(pm["qa2"]), qb=qpad(pm["qb"]),
                  ka=meta["ka"][None], vat=meta["vat"][None], kb=meta["kb"][None], vbt=meta["vbt"][None])
        oa, ob = attn(qm, None, q_pos=mq_pos, k_pos=mk_pos, k_valid=mk_valid, tq=LANES, tk=LANES)
        h_m, cst_m = _post(h_m, oa[:, :N_META], ob[:, :N_META], jnp.zeros((1, SUBLANES, ff2), F32), lw, N_META)

        pp = _projection(h_p, lw, tabs_p, tm_p, transposed_v=True)
        qp = dict(qa1=pp["qa1"], qa2=pp["qa2"], qb=pp["qb"], ka=pp["k16"], vat=pp["v16"], kb=pp["kb"],
                  vbt=pp["vb"])
        oa, ob = attn(qp, meta, q_pos=prompt_pos, k_pos=prompt_pos, k_valid=np.ones(seq, bool),
                      tq=tq_p, tk=tk_p)
        h_p, cst_p = _post(h_p, oa, ob, jnp.broadcast_to(cst_m, (bp, SUBLANES, ff2)), lw, tm_p)

        ps = _projection(h_s, lw, tabs_s, dec, transposed_v=False)
        kr_cache = jnp.pad(cache_b_krope[l], ((0, 0), (0, 0), (ROPE_LO, LANES - ROPE_LO - ROPE_B)))
        kb_c, vbt_c = _kv_up_cache(cache_b_ckv[l], kr_cache, lw, tm_c)
        ka_c = cache_a_k[l].reshape(bs, past, da).astype(BF16)
        vat_c = jnp.swapaxes(cache_a_v[l].reshape(bs, past, da).astype(BF16), 1, 2)
        qs = dict(qa1=qpad(ps["qa1"]), qa2=qpad(ps["qa2"]), qb=qpad(ps["qb"]),
                  ka=_pad_rows(jnp.concatenate([ka_c, ps["k16"]], axis=1), lk_s, 1),
                  vat=_pad_rows(jnp.concatenate([vat_c, jnp.swapaxes(ps["v16"], 1, 2)], axis=2), lk_s, 2),
                  kb=_pad_rows(jnp.concatenate([kb_c, ps["kb"]], axis=1), lk_s, 1),
                  vbt=_pad_rows(jnp.concatenate([vbt_c, jnp.swapaxes(ps["vb"], 1, 2)], axis=2), lk_s, 2))
        oa, ob = attn(qs, meta, q_pos=sq_pos, k_pos=sk_pos, k_valid=sk_valid, tq=LANES, tk=lk_s)
        h_s, cst_s = _post(h_s, oa[:, :dec], ob[:, :dec], _group_conv_state(state_ffn_conv[l], lw), lw, dec)

        bc = lambda a, n: jnp.broadcast_to(a, (n,) + a.shape[1:])
        heads = lambda a: a.reshape(a.shape[0], a.shape[1], H_A, 2 * HD_A)
        outs["ak_p"].append(jnp.concatenate([bc(heads(pm["kf"]), bp), heads(pp["kf"])], axis=1))
        outs["av_p"].append(jnp.concatenate([bc(heads(pm["vf"]), bp), heads(pp["vf"])], axis=1))
        outs["ck_p"].append(jnp.concatenate([bc(pm["ckv"], bp), pp["ckv"]], axis=1))
        outs["kr_p"].append(jnp.concatenate([bc(pm["kr"], bp), pp["kr"]], axis=1))
        outs["cv_p"].append(_ungroup_conv_state(cst_p, lw))
        outs["ak_s"].append(heads(ps["kf"]))
        outs["av_s"].append(heads(ps["vf"]))
        outs["ck_s"].append(ps["ckv"])
        outs["kr_s"].append(ps["kr"])
        outs["cv_s"].append(_ungroup_conv_state(cst_s, lw))

    y_p = _final_norm(h_p, final_norm_g, tm_p)
    y_s = _final_norm(h_s, final_norm_g, dec)
    st = lambda k: jnp.stack(outs[k])
    return (y_p, y_s, st("ak_p"), st("ak_s"), st("av_p"), st("av_s"), st("ck_p"), st("ck_s"),
            st("kr_p"), st("kr_s"), st("cv_p"), st("cv_s"))
```

```python
import functools
import math

import numpy as np
import jax
import jax.numpy as jnp
from jax import lax
from jax.experimental import pallas as pl
from jax.experimental.pallas import tpu as pltpu

CHUNK = 64
N_META = 16
EPS = 1e-6
H_A = 8
HD_A = 64
H_B = 8
NOPE_B = 64
ROPE_B = 32
V_B = 64
ROPE_THETA = 10000.0
NUM_BUCKETS = 32
MAX_DISTANCE = 128
CONV_W = 3

LANES = 128
SUBLANES = 8
LOG2E = 1.4426950408889634
MASKED = -1e30
FAR_REL = 91
VMEM_LIMIT = 56 * 1024 * 1024

F32 = jnp.float32
BF16 = jnp.bfloat16
HB_PAD = LANES
ROPE_LO = NOPE_B
ROPE_HALF = ROPE_B // 2


def _dot(a, b):
    return jnp.dot(a, b, preferred_element_type=F32)


def _dot_nt(a, b):
    return lax.dot_general(a, b, (((1,), (1,)), ((), ())), preferred_element_type=F32)


def _rms(x, g):
    return x * lax.rsqrt(jnp.mean(x * x, axis=-1, keepdims=True) + EPS) * g


def _rope_padded(x, cos, sin_lo, sin_hi):
    return (x * cos + pltpu.roll(x, LANES - ROPE_HALF, 1) * sin_lo
            + pltpu.roll(x, ROPE_HALF, 1) * sin_hi)


def _const_spec(shape):
    nd = len(shape)
    return pl.BlockSpec(shape, lambda *_: (0,) * nd, pipeline_mode=pl.Buffered(1))


def _proj_kernel(x_ref, g_ref, wq_ref, wk_ref, wv_ref, wv2_ref, wcq_ref, wckv_ref, wkr_ref, qng_ref,
                 wuq_ref, kvng_ref, wuk_ref, wuv_ref, cos_ref, slo_ref, shi_ref, cost_ref, sint_ref,
                 qa1_ref, qa2_ref, kf_ref, k16_ref, vf_ref, v16_ref, qb_ref, ckv_ref, kr_ref, kb_ref,
                 vb_ref, *, transposed):
    x = x_ref[...]
    hb = _rms(x, g_ref[...]).astype(BF16)
    scale_a = HD_A ** -0.5 * LOG2E
    scale_b = (NOPE_B + ROPE_B) ** -0.5 * LOG2E
    cos = cos_ref[...]
    slo = slo_ref[...]
    shi = shi_ref[...]

    if transposed:
        qa = _dot_nt(wq_ref[...], hb) * scale_a
        feat = lax.broadcasted_iota(jnp.int32, qa.shape, 0) % (2 * HD_A)
    else:
        qa = _dot(hb, wq_ref[...]) * scale_a
        feat = lax.broadcasted_iota(jnp.int32, qa.shape, 1) % (2 * HD_A)
    qa1_ref[...] = jnp.where(feat < HD_A, qa, 0.0).astype(BF16)
    qa2_ref[...] = jnp.where(feat >= HD_A, qa, 0.0).astype(BF16)

    k = _dot(hb, wk_ref[...])
    kf_ref[...] = k
    k16_ref[...] = k.astype(BF16)
    v = _dot(hb, wv_ref[...])
    vf_ref[...] = v
    if transposed:
        v16_ref[...] = _dot_nt(wv2_ref[...], hb).astype(BF16)
    else:
        v16_ref[...] = v.astype(BF16)

    cq = _rms(_dot(hb, wcq_ref[...]), qng_ref[...]).astype(BF16)
    if transposed:
        qb = _dot_nt(wuq_ref[...], cq) * scale_b
        ct = cost_ref[...]
        st = sint_ref[...]
        for h in range(H_B):
            lo = h * HB_PAD + ROPE_LO
            x1 = qb[lo:lo + ROPE_HALF]
            x2 = qb[lo + ROPE_HALF:lo + ROPE_B]
            qb_ref[h * HB_PAD:lo, :] = qb[h * HB_PAD:lo].astype(BF16)
            qb_ref[lo:lo + ROPE_HALF, :] = (x1 * ct - x2 * st).astype(BF16)
            qb_ref[lo + ROPE_HALF:lo + ROPE_B, :] = (x1 * st + x2 * ct).astype(BF16)
            qb_ref[lo + ROPE_B:(h + 1) * HB_PAD, :] = qb[lo + ROPE_B:(h + 1) * HB_PAD].astype(BF16)
    else:
        qb = _dot(cq, wuq_ref[...]) * scale_b
        for h in range(H_B):
            sl = slice(h * HB_PAD, (h + 1) * HB_PAD)
            qb_ref[:, sl] = _rope_padded(qb[:, sl], cos, slo, shi).astype(BF16)

    ckv = _rms(_dot(hb, wckv_ref[...]), kvng_ref[...])
    ckv_ref[...] = ckv
    kr = _rope_padded(_dot(hb, wkr_ref[...]), cos, slo, shi)
    kr_ref[...] = kr[:, ROPE_LO:ROPE_LO + ROPE_B]
    cb = ckv.astype(BF16)
    kn = _dot(cb, wuk_ref[...])
    for h in range(H_B):
        sl = slice(h * HB_PAD, (h + 1) * HB_PAD)
        kb_ref[:, sl] = (kn[:, sl] + kr).astype(BF16)
    if transposed:
        vb_ref[...] = _dot_nt(wuv_ref[...], cb).astype(BF16)
    else:
        vb_ref[...] = _dot(cb, wuv_ref[...]).astype(BF16)


def _projection(x, lw, tabs, tm, transposed):
    b, L, d = x.shape
    da = lw["wq"].shape[1]
    kvl = lw["wckv"].shape[1]
    hv = H_B * V_B
    hq = H_B * HB_PAD
    nj = L // tm
    row = lambda w: pl.BlockSpec((None, tm, w), lambda i, j: (i, j, 0))
    col = lambda w: pl.BlockSpec((None, w, tm), lambda i, j: (i, 0, j))
    tab = pl.BlockSpec((tm, LANES), lambda i, j: (j, 0))
    tab_t = pl.BlockSpec((ROPE_HALF, tm), lambda i, j: (0, j))
    t = "_t" if transposed else ""
    weights = [lw["g_mix"], lw["wq" + t], lw["wk"], lw["wv"], lw["wv_t"], lw["wcq"], lw["wckv"], lw["wkr"],
               lw["g_q"], lw["wuq" + t], lw["g_kv"], lw["wuk"], lw["wuv" + t]]
    in_specs = [row(d)] + [_const_spec(w.shape) for w in weights] + [tab, tab, tab, tab_t, tab_t]
    sds = jax.ShapeDtypeStruct
    feat = (lambda w: (sds((b, w, L), BF16), col(w))) if transposed else (lambda w: (sds((b, L, w), BF16), row(w)))
    outs = [feat(da), feat(da), (sds((b, L, da), F32), row(da)), (sds((b, L, da), BF16), row(da)),
            (sds((b, L, da), F32), row(da)), feat(da), feat(hq), (sds((b, L, kvl), F32), row(kvl)),
            (sds((b, L, ROPE_B), F32), row(ROPE_B)), (sds((b, L, hq), BF16), row(hq)), feat(hv)]
    res = pl.pallas_call(
        functools.partial(_proj_kernel, transposed=transposed),
        grid=(b, nj), in_specs=in_specs, out_specs=[s for _, s in outs], out_shape=[s for s, _ in outs],
        compiler_params=pltpu.CompilerParams(dimension_semantics=("parallel", "parallel"),
                                             vmem_limit_bytes=VMEM_LIMIT),
    )(x, *weights, *tabs)
    names = ("qa1", "qa2", "kf", "k16", "vf", "v16", "qb", "ckv", "kr", "kb", "vb")
    return dict(zip(names, res))


def _kvup_kernel(ckv_ref, kr_ref, wuk_ref, wuvt_ref, kb_ref, vbt_ref):
    cb = ckv_ref[...].astype(BF16)
    kn = _dot(cb, wuk_ref[...])
    kr = kr_ref[...]
    for h in range(H_B):
        sl = slice(h * HB_PAD, (h + 1) * HB_PAD)
        kb_ref[:, sl] = (kn[:, sl] + kr).astype(BF16)
    vbt_ref[...] = _dot_nt(wuvt_ref[...], cb).astype(BF16)


def _kv_up_cache(ckv, kr_pad, lw, tm):
    b, p, kvl = ckv.shape
    hv = H_B * V_B
    row = lambda w: pl.BlockSpec((None, tm, w), lambda i, j: (i, j, 0))
    return pl.pallas_call(
        _kvup_kernel, grid=(b, p // tm),
        in_specs=[row(kvl), row(LANES), _const_spec(lw["wuk"].shape), _const_spec(lw["wuv_t"].shape)],
        out_specs=[row(H_B * HB_PAD), pl.BlockSpec((None, hv, tm), lambda i, j: (i, 0, j))],
        out_shape=[jax.ShapeDtypeStruct((b, p, H_B * HB_PAD), BF16), jax.ShapeDtypeStruct((b, hv, p), BF16)],
        compiler_params=pltpu.CompilerParams(dimension_semantics=("parallel", "parallel"),
                                             vmem_limit_bytes=VMEM_LIMIT),
    )(ckv, kr_pad, lw["wuk"], lw["wuv_t"])


N_TILES = 2 * H_A + H_B
SCORE_LAG = 4
PV_LAG = 2
N_SCORE_SLOTS = SCORE_LAG + 2
N_PROB_SLOTS = PV_LAG + 2


def _softmax_tiles(tiles, tk, s_slots, p_slots, m_ref, l_ref, alpha_ref):
    n = len(tiles)

    def scores(t):
        k, q_t, _, bias, _, _ = tiles[t]
        st = _dot(k(), q_t()) + bias()
        s_slots[t % N_SCORE_SLOTS][0:tk, :] = st
        row = slice(t, t + 1)
        m_old = m_ref[row, :]
        m_new = jnp.maximum(m_old, jnp.max(st, axis=0, keepdims=True))
        m_ref[row, :] = m_new
        alpha_ref[row, :] = jnp.exp2(m_old - m_new)

    def probs(t):
        row = slice(t, t + 1)
        p = jnp.exp2(s_slots[t % N_SCORE_SLOTS][0:tk, :] - m_ref[row, :])
        l_ref[row, :] = alpha_ref[row, :] * l_ref[row, :] + jnp.sum(p, axis=0, keepdims=True)
        p_slots[t % N_PROB_SLOTS][0:tk, :] = p.astype(BF16)

    def values(t):
        _, _, v_t, _, acc_ref, idx = tiles[t]
        pv = _dot(v_t(), p_slots[t % N_PROB_SLOTS][0:tk, :])
        acc_ref[idx] = alpha_ref[t:t + 1, :] * acc_ref[idx] + pv

    for i in range(n + SCORE_LAG + PV_LAG):
        if i < n:
            scores(i)
        if 0 <= i - SCORE_LAG < n:
            probs(i - SCORE_LAG)
        if 0 <= i - SCORE_LAG - PV_LAG < n:
            values(i - SCORE_LAG - PV_LAG)


def _attn_kernel(qi_ref, ki_ref, var_ref, mvar_ref, flag_ref, *refs, has_meta, lam_init, tk):
    del qi_ref, ki_ref, var_ref, mvar_ref
    qa1_ref, qa2_ref, ka_ref, vat_ref, qb_ref, kb_ref, vbt_ref, bias_ref = refs[:8]
    refs = refs[8:]
    if has_meta:
        mka_ref, mvat_ref, mkb_ref, mvbt_ref, mbias_ref = refs[:5]
        refs = refs[5:]
    lam_ref, sg_ref, oa_ref, ob_ref, m_ref, l_ref, alpha_ref, acca, accb = refs[:9]
    s_slots = refs[9:9 + N_SCORE_SLOTS]
    p_slots = refs[9 + N_SCORE_SLOTS:]

    flags = flag_ref[pl.program_id(1)]
    first = (flags & 1) != 0
    last = (flags & 2) != 0

    def tiles_of(k_ref, vt_ref, k2_ref, vt2_ref, b_ref):
        tiles = []
        for h in range(H_A):
            sl = slice(h * 2 * HD_A, (h + 1) * 2 * HD_A)
            for mp, q_ref in enumerate((qa1_ref, qa2_ref)):
                tiles.append((lambda sl=sl: k_ref[:, sl], lambda sl=sl, q_ref=q_ref: q_ref[sl, :],
                              lambda sl=sl: vt_ref[sl, :], lambda h=h: b_ref[h], acca, 2 * h + mp))
        for h in range(H_B):
            sl = slice(h * HB_PAD, (h + 1) * HB_PAD)
            vsl = slice(h * V_B, (h + 1) * V_B)
            tiles.append((lambda sl=sl: k2_ref[:, sl], lambda sl=sl: qb_ref[sl, :],
                          lambda vsl=vsl: vt2_ref[vsl, :], lambda: b_ref[H_A], accb, h))
        return tiles

    @pl.when(first)
    def _():
        m_ref[...] = jnp.full(m_ref.shape, MASKED, F32)
        l_ref[...] = jnp.zeros(l_ref.shape, F32)
        acca[...] = jnp.zeros(acca.shape, F32)
        accb[...] = jnp.zeros(accb.shape, F32)
        if has_meta:
            _softmax_tiles(tiles_of(mka_ref, mvat_ref, mkb_ref, mvbt_ref, mbias_ref), LANES,
                           s_slots, p_slots, m_ref, l_ref, alpha_ref)

    _softmax_tiles(tiles_of(ka_ref, vat_ref, kb_ref, vbt_ref, bias_ref), tk,
                   s_slots, p_slots, m_ref, l_ref, alpha_ref)

    @pl.when(last)
    def _():
        al = lam_ref[...]
        lam = (jnp.exp(jnp.sum(al[0:1] * al[1:2], axis=1, keepdims=True))
               - jnp.exp(jnp.sum(al[2:3] * al[3:4], axis=1, keepdims=True)) + lam_init)
        sg = sg_ref[...]
        for h in range(H_A):
            o = (acca[2 * h] / l_ref[2 * h:2 * h + 1, :]
                 - lam * (acca[2 * h + 1] / l_ref[2 * h + 1:2 * h + 2, :]))
            y = o * lax.rsqrt(jnp.mean(o * o, axis=0, keepdims=True) + EPS) * sg * (1.0 - lam_init)
            oa_ref[:, h * 2 * HD_A:(h + 1) * 2 * HD_A] = y.T.astype(BF16)
        for j in range(H_B // 2):
            r = 2 * H_A + 2 * j
            pair = jnp.concatenate([accb[2 * j] / l_ref[r:r + 1, :],
                                    accb[2 * j + 1] / l_ref[r + 1:r + 2, :]], axis=0)
            ob_ref[:, j * 2 * V_B:(j + 1) * 2 * V_B] = pair.T.astype(BF16)


def _t5_bucket(rel):
    half = NUM_BUCKETS // 2
    max_exact = half // 2
    n = jnp.abs(rel)
    nf = jnp.maximum(n, 1).astype(F32)
    large = max_exact + (jnp.log(nf / max_exact) / math.log(MAX_DISTANCE / max_exact)
                         * (half - max_exact)).astype(jnp.int32)
    large = jnp.minimum(large, half - 1)
    return jnp.where(rel > 0, half, 0) + jnp.where(n < max_exact, n, large)


def _bias_tile(table, q_pos, k_pos, k_valid):
    q_pos = jnp.asarray(q_pos, jnp.int32)
    k_pos = jnp.asarray(k_pos, jnp.int32)
    rel = k_pos[:, None] - q_pos[None, :]
    bias = jnp.transpose(table.astype(F32)[_t5_bucket(rel)], (2, 0, 1)) * LOG2E
    bias = jnp.concatenate([bias, jnp.zeros((1,) + rel.shape, F32)], axis=0)
    visible = ((k_pos[:, None] // CHUNK) <= (q_pos[None, :] // CHUNK)) & jnp.asarray(k_valid)[:, None]
    return jnp.where(visible[None], bias, MASKED)


def _plan_tiles(q_pos, k_pos, k_valid, tq, tk, all_pairs):
    nq, nk = len(q_pos) // tq, len(k_pos) // tk
    pairs, variants, defs = [], {}, []
    for qi in range(nq):
        qp = q_pos[qi * tq:(qi + 1) * tq]
        for ki in range(nk):
            kp = k_pos[ki * tk:(ki + 1) * tk]
            kv = k_valid[ki * tk:(ki + 1) * tk]
            vis = ((kp[:, None] // CHUNK) <= (qp[None, :] // CHUNK)) & kv[:, None]
            if not (all_pairs or vis.any()):
                continue
            rel = kp[:, None] - qp[None, :]
            if vis.all() and rel.max() <= -FAR_REL:
                key = ("far",)
            else:
                key = (int(kp[0] - qp[0]), int(qp[0] % CHUNK), int(kp[0] % CHUNK), int(kv.sum()),
                       tuple(np.diff(qp).tolist()) if np.any(np.diff(qp) != 1) else ())
            if key not in variants:
                variants[key] = len(defs)
                defs.append((qp, kp, kv))
            pairs.append((qi, ki, variants[key]))
    return pairs, defs


def _attention(p, meta, table, a_lambda, subln_g, lam_init, q_pos, k_pos, k_valid, tq, tk):
    b, da, lq = p["qa1"].shape
    hv = H_B * V_B
    pairs, defs = _plan_tiles(q_pos, k_pos, k_valid, tq, tk, all_pairs=False)
    bias = jnp.stack([_bias_tile(table, *d) for d in defs])
    nq = lq // tq
    qi = np.array([a for a, _, _ in pairs], np.int32)
    ki = np.array([a for _, a, _ in pairs], np.int32)
    var = np.array([a for _, _, a in pairs], np.int32)
    flags = np.zeros(len(pairs), np.int32)
    for s in range(len(pairs)):
        if s == 0 or qi[s] != qi[s - 1]:
            flags[s] |= 1
        if s == len(pairs) - 1 or qi[s] != qi[s + 1]:
            flags[s] |= 2
    assert sorted(set(qi.tolist())) == list(range(nq))
    has_meta = meta is not None
    if has_meta:
        m_pos = np.concatenate([np.arange(-N_META, 0), np.zeros(LANES - N_META, np.int64)])
        m_valid = np.arange(LANES) < N_META
        mpairs, mdefs = _plan_tiles(q_pos, m_pos, m_valid, tq, LANES, all_pairs=True)
        mbias = jnp.stack([_bias_tile(table, *d) for d in mdefs])
        mvar_q = np.array([a for _, _, a in mpairs], np.int32)
        mvar = mvar_q[qi]
    else:
        mvar = np.zeros(len(pairs), np.int32)

    qblk = lambda w: pl.BlockSpec((None, tq, w), lambda i, s, qi_, ki_, v_, mv_, f_: (i, qi_[s], 0))
    qtblk = lambda w: pl.BlockSpec((None, w, tq), lambda i, s, qi_, ki_, v_, mv_, f_: (i, 0, qi_[s]))
    kblk = lambda w: pl.BlockSpec((None, tk, w), lambda i, s, qi_, ki_, v_, mv_, f_: (i, ki_[s], 0))
    ktblk = lambda w: pl.BlockSpec((None, w, tk), lambda i, s, qi_, ki_, v_, mv_, f_: (i, 0, ki_[s]))
    const = lambda shape: pl.BlockSpec(shape, lambda *_: (0,) * len(shape))
    slot_rows = max(tk, LANES)
    in_specs = [qtblk(da), qtblk(da), kblk(da), ktblk(da), qtblk(H_B * HB_PAD), kblk(H_B * HB_PAD), ktblk(hv),
                pl.BlockSpec((None, H_A + 1, tk, tq), lambda i, s, qi_, ki_, v_, mv_, f_: (v_[s], 0, 0, 0))]
    args = [p["qa1"], p["qa2"], p["ka"], p["vat"], p["qb"], p["kb"], p["vbt"], bias]
    if has_meta:
        in_specs += [const(meta["ka"].shape), const(meta["vat"].shape), const(meta["kb"].shape),
                     const(meta["vbt"].shape),
                     pl.BlockSpec((None, H_A + 1, LANES, tq),
                                  lambda i, s, qi_, ki_, v_, mv_, f_: (mv_[s], 0, 0, 0))]
        args += [meta["ka"], meta["vat"], meta["kb"], meta["vbt"], mbias]
    in_specs += [const(a_lambda.shape), const((2 * HD_A, 1))]
    args += [a_lambda, subln_g.reshape(2 * HD_A, 1)]
    grid_spec = pltpu.PrefetchScalarGridSpec(
        num_scalar_prefetch=5, grid=(b, len(pairs)), in_specs=in_specs,
        out_specs=[qblk(da), qblk(hv)],
        scratch_shapes=[pltpu.VMEM((N_TILES, tq), F32), pltpu.VMEM((N_TILES, tq), F32),
                        pltpu.VMEM((N_TILES, tq), F32),
                        pltpu.VMEM((2 * H_A, 2 * HD_A, tq), F32), pltpu.VMEM((H_B, V_B, tq), F32)]
        + [pltpu.VMEM((slot_rows, tq), F32)] * N_SCORE_SLOTS
        + [pltpu.VMEM((slot_rows, tq), BF16)] * N_PROB_SLOTS)
    oa, ob = pl.pallas_call(
        functools.partial(_attn_kernel, has_meta=has_meta, lam_init=lam_init, tk=tk),
        grid_spec=grid_spec,
        out_shape=[jax.ShapeDtypeStruct((b, lq, da), BF16), jax.ShapeDtypeStruct((b, lq, hv), BF16)],
        compiler_params=pltpu.CompilerParams(dimension_semantics=("parallel", "arbitrary"),
                                             vmem_limit_bytes=VMEM_LIMIT),
    )(jnp.asarray(qi), jnp.asarray(ki), jnp.asarray(var), jnp.asarray(mvar), jnp.asarray(flags), *args)
    return oa, ob


def _post_kernel(x_ref, oa_ref, ob_ref, gmix_ref, wg_ref, wabr_ref, wbbr_ref, wout_ref, gffn_ref, wup_ref,
                 cw_ref, cb_ref, wdn_ref, cbuf_ref, h_ref, cst_ref, acc_ref, ext_ref, carry_ref,
                 *, tm, fc, n_chunks):
    j = pl.program_id(1)
    d = x_ref.shape[-1]
    x = x_ref[...]
    hb = _rms(x, gmix_ref[...]).astype(BF16)
    gates = jax.nn.sigmoid(_dot(hb, wg_ref[...]))
    m = gates[:, :d] * _dot(oa_ref[...], wabr_ref[...]) + gates[:, d:] * _dot(ob_ref[...], wbbr_ref[...])
    h = x + _dot(m.astype(BF16), wout_ref[...])
    hb2 = _rms(h, gffn_ref[...]).astype(BF16)

    @pl.when(j == 0)
    def _():
        carry_ref[...] = cbuf_ref[...]

    acc_ref[...] = h
    for c in range(n_chunks):
        cols = slice(c * 2 * fc, (c + 1) * 2 * fc)
        u = _dot(hb2, wup_ref[:, cols])
        ext_ref[0:SUBLANES, :] = carry_ref[:, cols]
        ext_ref[SUBLANES:SUBLANES + tm, :] = u
        cw = cw_ref[:, cols]
        conv = (cb_ref[:, cols] + cw[0:1] * ext_ref[SUBLANES - 2:SUBLANES - 2 + tm, :]
                + cw[1:2] * ext_ref[SUBLANES - 1:SUBLANES - 1 + tm, :] + cw[2:3] * u)
        carry_ref[:, cols] = ext_ref[tm:tm + SUBLANES, :]
        a_up = conv[:, :fc]
        g_up = conv[:, fc:]
        act = (g_up * jax.nn.sigmoid(g_up) * a_up).astype(BF16)
        acc_ref[...] += _dot(act, wdn_ref[c * fc:(c + 1) * fc, :])
    h_ref[...] = acc_ref[...]

    @pl.when(j == pl.num_programs(1) - 1)
    def _():
        cst_ref[...] = carry_ref[...]


def _post(x, oa, ob, cbuf8, lw, tm):
    b, L, d = x.shape
    ff2 = lw["wup"].shape[1]
    fc = lw["fc"]
    n_chunks = ff2 // (2 * fc)
    row = lambda w: pl.BlockSpec((None, tm, w), lambda i, j: (i, j, 0))
    st = pl.BlockSpec((None, SUBLANES, ff2), lambda i, j: (i, 0, 0))
    weights = [lw["g_mix"], lw["wg"], lw["wabr"], lw["wbbr"], lw["wout"], lw["g_ffn"], lw["wup"],
               lw["conv_w"], lw["conv_b"], lw["wdn"]]
    return pl.pallas_call(
        functools.partial(_post_kernel, tm=tm, fc=fc, n_chunks=n_chunks),
        grid=(b, L // tm),
        in_specs=[row(d), row(oa.shape[-1]), row(ob.shape[-1])] + [_const_spec(w.shape) for w in weights] + [st],
        out_specs=[row(d), st],
        out_shape=[jax.ShapeDtypeStruct((b, L, d), F32), jax.ShapeDtypeStruct((b, SUBLANES, ff2), F32)],
        scratch_shapes=[pltpu.VMEM((tm, d), F32), pltpu.VMEM((tm + SUBLANES, 2 * fc), F32),
                        pltpu.VMEM((SUBLANES, ff2), F32)],
        compiler_params=pltpu.CompilerParams(dimension_semantics=("parallel", "arbitrary"),
                                             vmem_limit_bytes=VMEM_LIMIT),
    )(x, oa, ob, *weights, cbuf8)


def _norm_kernel(x_ref, g_ref, o_ref):
    o_ref[...] = _rms(x_ref[...], g_ref[...])


def _final_norm(x, g, tm):
    b, L, d = x.shape
    row = pl.BlockSpec((None, tm, d), lambda i, j: (i, j, 0))
    return pl.pallas_call(
        _norm_kernel, grid=(b, L // tm), in_specs=[row, _const_spec((1, d))], out_specs=row,
        out_shape=jax.ShapeDtypeStruct((b, L, d), F32),
        compiler_params=pltpu.CompilerParams(dimension_semantics=("parallel", "parallel")),
    )(x, g.reshape(1, d))


def _layer_weights(l, norm_mix_g, w_in, b_q_norm_g, w_b_uq, b_kv_norm_g, w_b_ukv, w_a_br, w_b_br, w_out,
                   norm_ffn_g, w_up, conv_w, conv_b, w_down):
    d = w_in.shape[1]
    da = H_A * 2 * HD_A
    q_lora = w_b_uq.shape[1]
    kvl = w_b_ukv.shape[1]
    d_ff = w_down.shape[1]
    w = w_in[l]
    o_cq = 3 * da
    o_ckv = o_cq + q_lora
    o_kr = o_ckv + kvl
    o_g = o_kr + ROPE_B
    bf = lambda a: a.astype(BF16)
    wv = w[:, 2 * da:3 * da]
    wkr = jnp.zeros((d, LANES), F32).at[:, ROPE_LO:ROPE_LO + ROPE_B].set(w[:, o_kr:o_g])
    wuq = jnp.pad(w_b_uq[l], ((0, 0), (0, 0), (0, HB_PAD - NOPE_B - ROPE_B))).reshape(q_lora, H_B * HB_PAD)
    wuk = jnp.pad(w_b_ukv[l][:, :, :NOPE_B], ((0, 0), (0, 0), (0, HB_PAD - NOPE_B))).reshape(kvl, H_B * HB_PAD)
    wuv = w_b_ukv[l][:, :, NOPE_B:].reshape(kvl, H_B * V_B)
    fc = 256 if d_ff % 256 == 0 else d_ff
    nch = d_ff // fc
    regroup = lambda a: a.reshape(a.shape[0], 2, nch, fc).transpose(0, 2, 1, 3).reshape(a.shape[0], 2 * d_ff)
    return dict(
        g_mix=norm_mix_g[l].reshape(1, d), wq=bf(w[:, :da]), wq_t=bf(w[:, :da].T), wk=bf(w[:, da:2 * da]),
        wv=bf(wv), wv_t=bf(wv.T),
        wcq=bf(w[:, o_cq:o_ckv]), wckv=bf(w[:, o_ckv:o_kr]), wkr=bf(wkr), wg=bf(w[:, o_g:]),
        g_q=b_q_norm_g[l].reshape(1, q_lora), wuq=bf(wuq), wuq_t=bf(wuq.T),
        g_kv=b_kv_norm_g[l].reshape(1, kvl), wuk=bf(wuk),
        wuv=bf(wuv), wuv_t=bf(wuv.T), wabr=bf(w_a_br[l]), wbbr=bf(w_b_br[l]), wout=bf(w_out[l]),
        g_ffn=norm_ffn_g[l].reshape(1, d), wup=bf(regroup(w_up[l])),
        conv_w=jnp.pad(regroup(conv_w[l]), ((0, SUBLANES - CONV_W), (0, 0))),
        conv_b=regroup(conv_b[l].reshape(1, 2 * d_ff)), wdn=bf(w_down[l]), fc=fc, nch=nch, d_ff=d_ff)


def _ungroup_conv_state(cst, lw):
    b = cst.shape[0]
    s = cst[:, SUBLANES - (CONV_W - 1):, :].reshape(b, CONV_W - 1, lw["nch"], 2, lw["fc"])
    return s.transpose(0, 1, 3, 2, 4).reshape(b, CONV_W - 1, 2 * lw["d_ff"])


def _group_conv_state(buf, lw):
    b = buf.shape[0]
    s = buf.reshape(b, CONV_W - 1, 2, lw["nch"], lw["fc"]).transpose(0, 1, 3, 2, 4)
    s = s.reshape(b, CONV_W - 1, 2 * lw["d_ff"])
    return jnp.pad(s, ((0, 0), (SUBLANES - (CONV_W - 1), 0), (0, 0)))


def _rope_tables(pos):
    inv = ROPE_THETA ** (-jnp.arange(ROPE_HALF, dtype=F32) / ROPE_HALF)
    ang = jnp.asarray(pos).astype(F32)[:, None] * inv[None, :]
    c, s = jnp.cos(ang), jnp.sin(ang)
    z = jnp.zeros_like(c)
    n = pos.shape[0]
    tail = jnp.zeros((n, LANES - ROPE_LO - ROPE_B), F32)
    head0 = jnp.zeros((n, ROPE_LO), F32)
    cos = jnp.concatenate([jnp.ones((n, ROPE_LO), F32), c, c, tail], axis=1)
    sin_lo = jnp.concatenate([head0, -s, z, tail], axis=1)
    sin_hi = jnp.concatenate([head0, z, s, tail], axis=1)
    return cos, sin_lo, sin_hi, c.T, s.T


def _pad_rows(a, n, axis):
    pad = [(0, 0)] * a.ndim
    pad[axis] = (0, n - a.shape[axis])
    return jnp.pad(a, pad)


def _row_tile(L, want):
    t = min(L, want)
    while L % t:
        t //= 2
    return t


def kernel(x_prompt, x_sample, cache_a_k, cache_a_v, cache_b_ckv, cache_b_krope, state_ffn_conv, meta_tokens,
           rel_bias_table, norm_mix_g, w_in, a_lambda, a_subln_g, b_q_norm_g, w_b_uq, b_kv_norm_g, w_b_ukv,
           w_a_br, w_b_br, w_out, norm_ffn_g, w_up, conv_w, conv_b, w_down, final_norm_g):
    bp, seq, d = x_prompt.shape
    bs, dec, _ = x_sample.shape
    depth = w_in.shape[0]
    past = cache_a_k.shape[2]
    da = H_A * 2 * HD_A
    ff2 = w_up.shape[2]
    assert seq % CHUNK == 0 and N_META % SUBLANES == 0 and dec % SUBLANES == 0

    meta_pos = np.arange(-N_META, 0)
    prompt_pos = np.arange(seq)
    sample_pos = np.arange(past, past + dec)
    tabs_m, tabs_p, tabs_s = _rope_tables(meta_pos), _rope_tables(prompt_pos), _rope_tables(sample_pos)

    tm_p = _row_tile(seq, 256)
    tq_p = tk_p = _row_tile(seq, 256)
    qpad = lambda a: _pad_rows(jnp.swapaxes(a, 1, 2), LANES, 2)
    mq_pos = np.concatenate([meta_pos, np.full(LANES - N_META, -1)])
    mk_pos = mq_pos
    mk_valid = np.arange(LANES) < N_META
    sq_pos = np.concatenate([sample_pos, np.full(LANES - dec, sample_pos[-1])])
    lk_s = -(-(past + dec) // LANES) * LANES
    sk_pos = np.concatenate([np.arange(past + dec), np.zeros(lk_s - past - dec, np.int64)])
    sk_valid = np.arange(lk_s) < past + dec
    tm_c = _row_tile(past, 256)

    h_m = meta_tokens[None].astype(F32)
    h_p, h_s = x_prompt, x_sample
    outs = {k: [] for k in ("ak_p", "ak_s", "av_p", "av_s", "ck_p", "ck_s", "kr_p", "kr_s", "cv_p", "cv_s")}
    for l in range(depth):
        lam_init = 0.8 - 0.6 * math.exp(-0.3 * l)
        lw = _layer_weights(l, norm_mix_g, w_in, b_q_norm_g, w_b_uq, b_kv_norm_g, w_b_ukv, w_a_br, w_b_br,
                            w_out, norm_ffn_g, w_up, conv_w, conv_b, w_down)
        attn = functools.partial(_attention, table=rel_bias_table, a_lambda=a_lambda[l],
                                 subln_g=a_subln_g[l], lam_init=lam_init)

        pm = _projection(h_m, lw, tabs_m, N_META, transposed=False)
        meta = dict(ka=_pad_rows(pm["k16"][0], LANES, 0), vat=_pad_rows(pm["v16"][0].T, LANES, 1),
                    kb=_pad_rows(pm["kb"][0], LANES, 0), vbt=_pad_rows(pm["vb"][0].T, LANES, 1))
        qm = dict(qa1=qpad(pm["qa1"]), qa2=qpad(pm["qa2"]), qb=qpad(pm["qb"]),
                  ka=meta["ka"][None], vat=meta["vat"][None], kb=meta["kb"][None], vbt=meta["vbt"][None])
        oa, ob = attn(qm, None, q_pos=mq_pos, k_pos=mk_pos, k_valid=mk_valid, tq=LANES, tk=LANES)
        h_m, cst_m = _post(h_m, oa[:, :N_META], ob[:, :N_META], jnp.zeros((1, SUBLANES, ff2), F32), lw, N_META)

        pp = _projection(h_p, lw, tabs_p, tm_p, transposed=True)
        qp = dict(qa1=pp["qa1"], qa2=pp["qa2"], qb=pp["qb"], ka=pp["k16"], vat=pp["v16"], kb=pp["kb"],
                  vbt=pp["vb"])
        oa, ob = attn(qp, meta, q_pos=prompt_pos, k_pos=prompt_pos, k_valid=np.ones(seq, bool),
                      tq=tq_p, tk=tk_p)
        h_p, cst_p = _post(h_p, oa, ob, jnp.broadcast_to(cst_m, (bp, SUBLANES, ff2)), lw, tm_p)

        ps = _projection(h_s, lw, tabs_s, dec, transposed=False)
        kr_cache = jnp.pad(cache_b_krope[l], ((0, 0), (0, 0), (ROPE_LO, LANES - ROPE_LO - ROPE_B)))
        kb_c, vbt_c = _kv_up_cache(cache_b_ckv[l], kr_cache, lw, tm_c)
        ka_c = cache_a_k[l].reshape(bs, past, da).astype(BF16)
        vat_c = jnp.swapaxes(cache_a_v[l].reshape(bs, past, da).astype(BF16), 1, 2)
        qs = dict(qa1=qpad(ps["qa1"]), qa2=qpad(ps["qa2"]), qb=qpad(ps["qb"]),
                  ka=_pad_rows(jnp.concatenate([ka_c, ps["k16"]], axis=1), lk_s, 1),
                  vat=_pad_rows(jnp.concatenate([vat_c, jnp.swapaxes(ps["v16"], 1, 2)], axis=2), lk_s, 2),
                  kb=_pad_rows(jnp.concatenate([kb_c, ps["kb"]], axis=1), lk_s, 1),
                  vbt=_pad_rows(jnp.concatenate([vbt_c, jnp.swapaxes(ps["vb"], 1, 2)], axis=2), lk_s, 2))
        oa, ob = attn(qs, meta, q_pos=sq_pos, k_pos=sk_pos, k_valid=sk_valid, tq=LANES, tk=lk_s)
        h_s, cst_s = _post(h_s, oa[:, :dec], ob[:, :dec], _group_conv_state(state_ffn_conv[l], lw), lw, dec)

        bc = lambda a, n: jnp.broadcast_to(a, (n,) + a.shape[1:])
        heads = lambda a: a.reshape(a.shape[0], a.shape[1], H_A, 2 * HD_A)
        outs["ak_p"].append(jnp.concatenate([bc(heads(pm["kf"]), bp), heads(pp["kf"])], axis=1))
        outs["av_p"].append(jnp.concatenate([bc(heads(pm["vf"]), bp), heads(pp["vf"])], axis=1))
        outs["ck_p"].append(jnp.concatenate([bc(pm["ckv"], bp), pp["ckv"]], axis=1))
        outs["kr_p"].append(jnp.concatenate([bc(pm["kr"], bp), pp["kr"]], axis=1))
        outs["cv_p"].append(_ungroup_conv_state(cst_p, lw))
        outs["ak_s"].append(heads(ps["kf"]))
        outs["av_s"].append(heads(ps["vf"]))
        outs["ck_s"].append(ps["ckv"])
        outs["kr_s"].append(ps["kr"])
        outs["cv_s"].append(_ungroup_conv_state(cst_s, lw))

    y_p = _final_norm(h_p, final_norm_g, tm_p)
    y_s = _final_norm(h_s, final_norm_g, dec)
    st = lambda k: jnp.stack(outs[k])
    return (y_p, y_s, st("ak_p"), st("ak_s"), st("av_p"), st("av_s"), st("ck_p"), st("ck_s"),
            st("kr_p"), st("kr_s"), st("cv_p"), st("cv_s"))
```

```python
import functools
import math

import numpy as np
import jax
import jax.numpy as jnp
from jax import lax
from jax.experimental import pallas as pl
from jax.experimental.pallas import tpu as pltpu

CHUNK = 64
N_META = 16
EPS = 1e-6
H_A = 8
HD_A = 64
H_B = 8
NOPE_B = 64
ROPE_B = 32
V_B = 64
ROPE_THETA = 10000.0
NUM_BUCKETS = 32
MAX_DISTANCE = 128
CONV_W = 3

LANES = 128
SUBLANES = 8
BF16_ROWS = 16
LOG2E = 1.4426950408889634
MASKED = -1e30
FAR_REL = 91
VMEM_LIMIT = 56 * 1024 * 1024

F32 = jnp.float32
BF16 = jnp.bfloat16
HB_PAD = LANES
ROPE_LO = NOPE_B
ROPE_HALF = ROPE_B // 2
DV_A = 2 * HD_A
VA_PAD = DV_A + BF16_ROWS
VB_PAD = V_B + BF16_ROWS


def _dot(a, b):
    return jnp.dot(a, b, preferred_element_type=F32)


def _dot_nt(a, b):
    return lax.dot_general(a, b, (((1,), (1,)), ((), ())), preferred_element_type=F32)


def _rms(x, g):
    return x * lax.rsqrt(jnp.mean(x * x, axis=-1, keepdims=True) + EPS) * g


def _rope_padded(x, cos, sin_lo, sin_hi):
    return (x * cos + pltpu.roll(x, LANES - ROPE_HALF, 1) * sin_lo
            + pltpu.roll(x, ROPE_HALF, 1) * sin_hi)


def _const_spec(shape):
    nd = len(shape)
    return pl.BlockSpec(shape, lambda *_: (0,) * nd, pipeline_mode=pl.Buffered(1))


def _sum_rows(n):
    r = lax.broadcasted_iota(jnp.int32, (BF16_ROWS, n), 0)
    return jnp.where(r == 0, 1.0, 0.0).astype(BF16)


def _proj_kernel(x_ref, g_ref, wq_ref, wk_ref, wv_ref, wv2_ref, wcq_ref, wckv_ref, wkr_ref, qng_ref,
                 wuq_ref, kvng_ref, wuk_ref, wuv_ref, cos_ref, slo_ref, shi_ref, cost_ref, sint_ref,
                 qa1_ref, qa2_ref, kf_ref, k16_ref, vf_ref, v16_ref, qb_ref, ckv_ref, kr_ref, kb_ref,
                 vb_ref, *, transposed):
    x = x_ref[...]
    rows = x.shape[0]
    hb = _rms(x, g_ref[...]).astype(BF16)
    scale_a = HD_A ** -0.5 * LOG2E
    scale_b = (NOPE_B + ROPE_B) ** -0.5 * LOG2E
    cos = cos_ref[...]
    slo = slo_ref[...]
    shi = shi_ref[...]

    if transposed:
        qa = _dot_nt(wq_ref[...], hb) * scale_a
        feat = lax.broadcasted_iota(jnp.int32, qa.shape, 0) % (2 * HD_A)
    else:
        qa = _dot(hb, wq_ref[...]) * scale_a
        feat = lax.broadcasted_iota(jnp.int32, qa.shape, 1) % (2 * HD_A)
    qa1_ref[...] = jnp.where(feat < HD_A, qa, 0.0).astype(BF16)
    qa2_ref[...] = jnp.where(feat >= HD_A, qa, 0.0).astype(BF16)

    k = _dot(hb, wk_ref[...])
    kf_ref[...] = k
    k16_ref[...] = k.astype(BF16)
    v = _dot(hb, wv_ref[...])
    vf_ref[...] = v
    if transposed:
        vt = _dot_nt(wv2_ref[...], hb).astype(BF16)
        ones = _sum_rows(rows)
        for h in range(H_A):
            v16_ref[h * VA_PAD:h * VA_PAD + DV_A, :] = vt[h * DV_A:(h + 1) * DV_A]
            v16_ref[h * VA_PAD + DV_A:(h + 1) * VA_PAD, :] = ones
    else:
        v16_ref[...] = v.astype(BF16)

    cq = _rms(_dot(hb, wcq_ref[...]), qng_ref[...]).astype(BF16)
    if transposed:
        qb = _dot_nt(wuq_ref[...], cq) * scale_b
        ct = cost_ref[...]
        st = sint_ref[...]
        for h in range(H_B):
            lo = h * HB_PAD + ROPE_LO
            x1 = qb[lo:lo + ROPE_HALF]
            x2 = qb[lo + ROPE_HALF:lo + ROPE_B]
            qb_ref[h * HB_PAD:lo, :] = qb[h * HB_PAD:lo].astype(BF16)
            qb_ref[lo:lo + ROPE_HALF, :] = (x1 * ct - x2 * st).astype(BF16)
            qb_ref[lo + ROPE_HALF:lo + ROPE_B, :] = (x1 * st + x2 * ct).astype(BF16)
            qb_ref[lo + ROPE_B:(h + 1) * HB_PAD, :] = qb[lo + ROPE_B:(h + 1) * HB_PAD].astype(BF16)
    else:
        qb = _dot(cq, wuq_ref[...]) * scale_b
        for h in range(H_B):
            sl = slice(h * HB_PAD, (h + 1) * HB_PAD)
            qb_ref[:, sl] = _rope_padded(qb[:, sl], cos, slo, shi).astype(BF16)

    ckv = _rms(_dot(hb, wckv_ref[...]), kvng_ref[...])
    ckv_ref[...] = ckv
    kr = _rope_padded(_dot(hb, wkr_ref[...]), cos, slo, shi)
    kr_ref[...] = kr[:, ROPE_LO:ROPE_LO + ROPE_B]
    cb = ckv.astype(BF16)
    kn = _dot(cb, wuk_ref[...])
    for h in range(H_B):
        sl = slice(h * HB_PAD, (h + 1) * HB_PAD)
        kb_ref[:, sl] = (kn[:, sl] + kr).astype(BF16)
    if transposed:
        vbt = _dot_nt(wuv_ref[...], cb).astype(BF16)
        ones = _sum_rows(rows)
        for h in range(H_B):
            vb_ref[h * VB_PAD:h * VB_PAD + V_B, :] = vbt[h * V_B:(h + 1) * V_B]
            vb_ref[h * VB_PAD + V_B:(h + 1) * VB_PAD, :] = ones
    else:
        vb_ref[...] = _dot(cb, wuv_ref[...]).astype(BF16)


def _projection(x, lw, tabs, tm, transposed):
    b, L, d = x.shape
    da = lw["wq"].shape[1]
    kvl = lw["wckv"].shape[1]
    hv = H_B * V_B
    hq = H_B * HB_PAD
    nj = L // tm
    row = lambda w: pl.BlockSpec((None, tm, w), lambda i, j: (i, j, 0))
    col = lambda w: pl.BlockSpec((None, w, tm), lambda i, j: (i, 0, j))
    tab = pl.BlockSpec((tm, LANES), lambda i, j: (j, 0))
    tab_t = pl.BlockSpec((ROPE_HALF, tm), lambda i, j: (0, j))
    t = "_t" if transposed else ""
    weights = [lw["g_mix"], lw["wq" + t], lw["wk"], lw["wv"], lw["wv_t"], lw["wcq"], lw["wckv"], lw["wkr"],
               lw["g_q"], lw["wuq" + t], lw["g_kv"], lw["wuk"], lw["wuv" + t]]
    in_specs = [row(d)] + [_const_spec(w.shape) for w in weights] + [tab, tab, tab, tab_t, tab_t]
    sds = jax.ShapeDtypeStruct
    if transposed:
        feat = lambda w, wt: (sds((b, wt, L), BF16), col(wt))
    else:
        feat = lambda w, wt: (sds((b, L, w), BF16), row(w))
    outs = [feat(da, da), feat(da, da), (sds((b, L, da), F32), row(da)), (sds((b, L, da), BF16), row(da)),
            (sds((b, L, da), F32), row(da)), feat(da, H_A * VA_PAD), feat(hq, hq),
            (sds((b, L, kvl), F32), row(kvl)), (sds((b, L, ROPE_B), F32), row(ROPE_B)),
            (sds((b, L, hq), BF16), row(hq)), feat(hv, H_B * VB_PAD)]
    res = pl.pallas_call(
        functools.partial(_proj_kernel, transposed=transposed),
        grid=(b, nj), in_specs=in_specs, out_specs=[s for _, s in outs], out_shape=[s for s, _ in outs],
        compiler_params=pltpu.CompilerParams(dimension_semantics=("parallel", "parallel"),
                                             vmem_limit_bytes=VMEM_LIMIT),
    )(x, *weights, *tabs)
    names = ("qa1", "qa2", "kf", "k16", "vf", "v16", "qb", "ckv", "kr", "kb", "vb")
    return dict(zip(names, res))


def _kvup_kernel(ckv_ref, kr_ref, wuk_ref, wuvt_ref, kb_ref, vbt_ref):
    cb = ckv_ref[...].astype(BF16)
    kn = _dot(cb, wuk_ref[...])
    kr = kr_ref[...]
    for h in range(H_B):
        sl = slice(h * HB_PAD, (h + 1) * HB_PAD)
        kb_ref[:, sl] = (kn[:, sl] + kr).astype(BF16)
    vbt_ref[...] = _dot_nt(wuvt_ref[...], cb).astype(BF16)


def _kv_up_cache(ckv, kr_pad, lw, tm):
    b, p, kvl = ckv.shape
    hv = H_B * V_B
    row = lambda w: pl.BlockSpec((None, tm, w), lambda i, j: (i, j, 0))
    return pl.pallas_call(
        _kvup_kernel, grid=(b, p // tm),
        in_specs=[row(kvl), row(LANES), _const_spec(lw["wuk"].shape), _const_spec(lw["wuv_t"].shape)],
        out_specs=[row(H_B * HB_PAD), pl.BlockSpec((None, hv, tm), lambda i, j: (i, 0, j))],
        out_shape=[jax.ShapeDtypeStruct((b, p, H_B * HB_PAD), BF16), jax.ShapeDtypeStruct((b, hv, p), BF16)],
        compiler_params=pltpu.CompilerParams(dimension_semantics=("parallel", "parallel"),
                                             vmem_limit_bytes=VMEM_LIMIT),
    )(ckv, kr_pad, lw["wuk"], lw["wuv_t"])


N_TILES = 2 * H_A + H_B
SCORE_LAG = 4
PV_LAG = 2
N_SCORE_SLOTS = SCORE_LAG + 2
N_PROB_SLOTS = PV_LAG + 2
STEP_FIRST, STEP_LAST, STEP_BIASED = 1, 2, 4


def _softmax_tiles(tiles, tk, s_slots, p_slots, m_ref, alpha_ref):
    n = len(tiles)

    def scores(t):
        k, q_t, _, bias, _, _ = tiles[t]
        st = _dot(k(), q_t())
        if bias is not None:
            st = st + bias()
        s_slots[t % N_SCORE_SLOTS][0:tk, :] = st
        row = slice(t, t + 1)
        m_old = m_ref[row, :]
        m_new = jnp.maximum(m_old, jnp.max(st, axis=0, keepdims=True))
        m_ref[row, :] = m_new
        alpha_ref[row, :] = jnp.exp2(m_old - m_new)

    def probs(t):
        p = jnp.exp2(s_slots[t % N_SCORE_SLOTS][0:tk, :] - m_ref[t:t + 1, :])
        p_slots[t % N_PROB_SLOTS][0:tk, :] = p.astype(BF16)

    def values(t):
        _, _, v_t, _, acc_ref, idx = tiles[t]
        pv = _dot(v_t(), p_slots[t % N_PROB_SLOTS][0:tk, :])
        acc_ref[idx] = alpha_ref[t:t + 1, :] * acc_ref[idx] + pv

    for i in range(n + SCORE_LAG + PV_LAG):
        if i < n:
            scores(i)
        if 0 <= i - SCORE_LAG < n:
            probs(i - SCORE_LAG)
        if 0 <= i - SCORE_LAG - PV_LAG < n:
            values(i - SCORE_LAG - PV_LAG)


def _attn_kernel(qi_ref, ki_ref, var_ref, mvar_ref, flag_ref, *refs, has_meta, lam_init, tk):
    del qi_ref, ki_ref
    qa1_ref, qa2_ref, ka_ref, vat_ref, qb_ref, kb_ref, vbt_ref, bias_ref = refs[:8]
    refs = refs[8:]
    if has_meta:
        mka_ref, mvat_ref, mkb_ref, mvbt_ref, mbias_ref = refs[:5]
        refs = refs[5:]
    lam_ref, sg_ref, oa_ref, ob_ref, m_ref, alpha_ref, acca, accb = refs[:8]
    s_slots = refs[8:8 + N_SCORE_SLOTS]
    p_slots = refs[8 + N_SCORE_SLOTS:]

    step = pl.program_id(1)
    flags = flag_ref[step]
    first = (flags & STEP_FIRST) != 0
    last = (flags & STEP_LAST) != 0
    biased = (flags & STEP_BIASED) != 0

    def tiles_of(k_ref, vt_ref, k2_ref, vt2_ref, b_ref, variant):
        tiles = []
        for h in range(H_A):
            sl = slice(h * 2 * HD_A, (h + 1) * 2 * HD_A)
            vsl = slice(h * VA_PAD, (h + 1) * VA_PAD)
            bias = None if b_ref is None else (lambda h=h: b_ref[variant, h])
            for mp, q_ref in enumerate((qa1_ref, qa2_ref)):
                tiles.append((lambda sl=sl: k_ref[:, sl], lambda sl=sl, q_ref=q_ref: q_ref[sl, :],
                              lambda vsl=vsl: vt_ref[vsl, :], bias, acca, 2 * h + mp))
        bias = None if b_ref is None else (lambda: b_ref[variant, H_A])
        for h in range(H_B):
            sl = slice(h * HB_PAD, (h + 1) * HB_PAD)
            vsl = slice(h * VB_PAD, (h + 1) * VB_PAD)
            tiles.append((lambda sl=sl: k2_ref[:, sl], lambda sl=sl: qb_ref[sl, :],
                          lambda vsl=vsl: vt2_ref[vsl, :], bias, accb, h))
        return tiles

    @pl.when(first)
    def _():
        m_ref[...] = jnp.full(m_ref.shape, MASKED, F32)
        acca[...] = jnp.zeros(acca.shape, F32)
        accb[...] = jnp.zeros(accb.shape, F32)
        if has_meta:
            _softmax_tiles(tiles_of(mka_ref, mvat_ref, mkb_ref, mvbt_ref, mbias_ref, mvar_ref[step]), LANES,
                           s_slots, p_slots, m_ref, alpha_ref)

    @pl.when(biased)
    def _():
        _softmax_tiles(tiles_of(ka_ref, vat_ref, kb_ref, vbt_ref, bias_ref, var_ref[step]), tk,
                       s_slots, p_slots, m_ref, alpha_ref)

    @pl.when(jnp.logical_not(biased))
    def _():
        _softmax_tiles(tiles_of(ka_ref, vat_ref, kb_ref, vbt_ref, None, None), tk,
                       s_slots, p_slots, m_ref, alpha_ref)

    @pl.when(last)
    def _():
        al = lam_ref[...]
        lam = (jnp.exp(jnp.sum(al[0:1] * al[1:2], axis=1, keepdims=True))
               - jnp.exp(jnp.sum(al[2:3] * al[3:4], axis=1, keepdims=True)) + lam_init)
        sg = sg_ref[...]

        def normalised(acc_ref, idx, dv):
            a = acc_ref[idx]
            return a[0:dv] / a[dv:dv + 1]

        for h in range(H_A):
            o = normalised(acca, 2 * h, DV_A) - lam * normalised(acca, 2 * h + 1, DV_A)
            y = o * lax.rsqrt(jnp.mean(o * o, axis=0, keepdims=True) + EPS) * sg * (1.0 - lam_init)
            oa_ref[:, h * DV_A:(h + 1) * DV_A] = y.T.astype(BF16)
        for j in range(H_B // 2):
            pair = jnp.concatenate([normalised(accb, 2 * j, V_B), normalised(accb, 2 * j + 1, V_B)],
                                   axis=0)
            ob_ref[:, j * 2 * V_B:(j + 1) * 2 * V_B] = pair.T.astype(BF16)


def _t5_bucket(rel):
    half = NUM_BUCKETS // 2
    max_exact = half // 2
    n = jnp.abs(rel)
    nf = jnp.maximum(n, 1).astype(F32)
    large = max_exact + (jnp.log(nf / max_exact) / math.log(MAX_DISTANCE / max_exact)
                         * (half - max_exact)).astype(jnp.int32)
    large = jnp.minimum(large, half - 1)
    return jnp.where(rel > 0, half, 0) + jnp.where(n < max_exact, n, large)


def _bias_tile(table, q_pos, k_pos, k_valid):
    q_pos = jnp.asarray(q_pos, jnp.int32)
    k_pos = jnp.asarray(k_pos, jnp.int32)
    rel = k_pos[:, None] - q_pos[None, :]
    table = table.astype(F32)
    onehot = (_t5_bucket(rel)[..., None] == jnp.arange(NUM_BUCKETS)).astype(F32)
    shifted = (table - table[NUM_BUCKETS // 2 - 1]) * LOG2E
    bias = jnp.einsum("kqb,bh->hkq", onehot, shifted, precision=lax.Precision.HIGHEST)
    bias = jnp.concatenate([bias, jnp.zeros((1,) + rel.shape, F32)], axis=0)
    visible = ((k_pos[:, None] // CHUNK) <= (q_pos[None, :] // CHUNK)) & jnp.asarray(k_valid)[:, None]
    return jnp.where(visible[None], bias, MASKED)


def _plan_tiles(q_pos, k_pos, k_valid, tq, tk, all_pairs):
    nq, nk = len(q_pos) // tq, len(k_pos) // tk
    pairs, variants, defs = [], {}, []
    for qi in range(nq):
        qp = q_pos[qi * tq:(qi + 1) * tq]
        for ki in range(nk):
            kp = k_pos[ki * tk:(ki + 1) * tk]
            kv = k_valid[ki * tk:(ki + 1) * tk]
            vis = ((kp[:, None] // CHUNK) <= (qp[None, :] // CHUNK)) & kv[:, None]
            if not (all_pairs or vis.any()):
                continue
            rel = kp[:, None] - qp[None, :]
            if vis.all() and rel.max() <= -FAR_REL and not all_pairs:
                pairs.append((qi, ki, -1))
                continue
            key = (int(kp[0] - qp[0]), int(qp[0] % CHUNK), int(kp[0] % CHUNK), int(kv.sum()),
                   tuple(np.diff(qp).tolist()) if np.any(np.diff(qp) != 1) else ())
            if key not in variants:
                variants[key] = len(defs)
                defs.append((qp, kp, kv))
            pairs.append((qi, ki, variants[key]))
    return pairs, defs


def _attention_plan(table, q_pos, k_pos, k_valid, tq, tk, has_meta):
    pairs, defs = _plan_tiles(q_pos, k_pos, k_valid, tq, tk, all_pairs=False)
    if not defs:
        defs = [(q_pos[:tq], k_pos[:tk], k_valid[:tk])]
    bias = jnp.stack([_bias_tile(table, *d) for d in defs])
    qi = np.array([a for a, _, _ in pairs], np.int32)
    ki = np.array([a for _, a, _ in pairs], np.int32)
    var = np.array([a for _, _, a in pairs], np.int32)
    flags = np.zeros(len(pairs), np.int32)
    for s in range(len(pairs)):
        if s == 0 or qi[s] != qi[s - 1]:
            flags[s] |= STEP_FIRST
        if s == len(pairs) - 1 or qi[s] != qi[s + 1]:
            flags[s] |= STEP_LAST
        if var[s] >= 0:
            flags[s] |= STEP_BIASED
    assert sorted(set(qi.tolist())) == list(range(len(q_pos) // tq))
    plan = dict(tq=tq, tk=tk, qi=qi, ki=ki, var=np.maximum(var, 0), flags=flags, bias=bias,
                mvar=np.zeros(len(pairs), np.int32), mbias=None)
    if has_meta:
        m_pos = np.concatenate([np.arange(-N_META, 0), np.zeros(LANES - N_META, np.int64)])
        m_valid = np.arange(LANES) < N_META
        mpairs, mdefs = _plan_tiles(q_pos, m_pos, m_valid, tq, LANES, all_pairs=True)
        plan["mbias"] = jnp.stack([_bias_tile(table, *d) for d in mdefs])
        plan["mvar"] = np.array([a for _, _, a in mpairs], np.int32)[qi]
    return plan


def _attention(p, meta, plan, a_lambda, subln_g, lam_init):
    b, da, lq = p["qa1"].shape
    hv = H_B * V_B
    tq, tk = plan["tq"], plan["tk"]
    n_steps = len(plan["qi"])
    has_meta = meta is not None

    qblk = lambda w: pl.BlockSpec((None, tq, w), lambda i, s, qi_, ki_, v_, mv_, f_: (i, qi_[s], 0))
    qtblk = lambda w: pl.BlockSpec((None, w, tq), lambda i, s, qi_, ki_, v_, mv_, f_: (i, 0, qi_[s]))
    kblk = lambda w: pl.BlockSpec((None, tk, w), lambda i, s, qi_, ki_, v_, mv_, f_: (i, ki_[s], 0))
    ktblk = lambda w: pl.BlockSpec((None, w, tk), lambda i, s, qi_, ki_, v_, mv_, f_: (i, 0, ki_[s]))
    slot_rows = max(tk, LANES)
    in_specs = [qtblk(da), qtblk(da), kblk(da), ktblk(H_A * VA_PAD), qtblk(H_B * HB_PAD), kblk(H_B * HB_PAD),
                ktblk(H_B * VB_PAD), _const_spec(plan["bias"].shape)]
    args = [p["qa1"], p["qa2"], p["ka"], p["vat"], p["qb"], p["kb"], p["vbt"], plan["bias"]]
    if has_meta:
        margs = [meta["ka"], meta["vat"], meta["kb"], meta["vbt"], plan["mbias"]]
        in_specs += [_const_spec(a.shape) for a in margs]
        args += margs
    in_specs += [_const_spec(a_lambda.shape), _const_spec((DV_A, 1))]
    args += [a_lambda, subln_g.reshape(DV_A, 1)]
    grid_spec = pltpu.PrefetchScalarGridSpec(
        num_scalar_prefetch=5, grid=(b, n_steps), in_specs=in_specs,
        out_specs=[qblk(da), qblk(hv)],
        scratch_shapes=[pltpu.VMEM((N_TILES, tq), F32), pltpu.VMEM((N_TILES, tq), F32),
                        pltpu.VMEM((2 * H_A, VA_PAD, tq), F32), pltpu.VMEM((H_B, VB_PAD, tq), F32)]
        + [pltpu.VMEM((slot_rows, tq), F32)] * N_SCORE_SLOTS
        + [pltpu.VMEM((slot_rows, tq), BF16)] * N_PROB_SLOTS)
    oa, ob = pl.pallas_call(
        functools.partial(_attn_kernel, has_meta=has_meta, lam_init=lam_init, tk=tk),
        grid_spec=grid_spec,
        out_shape=[jax.ShapeDtypeStruct((b, lq, da), BF16), jax.ShapeDtypeStruct((b, lq, hv), BF16)],
        compiler_params=pltpu.CompilerParams(dimension_semantics=("parallel", "arbitrary"),
                                             vmem_limit_bytes=VMEM_LIMIT),
    )(*(jnp.asarray(plan[k]) for k in ("qi", "ki", "var", "mvar", "flags")), *args)
    return oa, ob


def _post_kernel(x_ref, oa_ref, ob_ref, gmix_ref, wg_ref, wabr_ref, wbbr_ref, wout_ref, gffn_ref, wup_ref,
                 cw_ref, cb_ref, wdn_ref, cbuf_ref, h_ref, cst_ref, acc_ref, ext_ref, carry_ref,
                 *, tm, fc, n_chunks):
    j = pl.program_id(1)
    d = x_ref.shape[-1]
    x = x_ref[...]
    hb = _rms(x, gmix_ref[...]).astype(BF16)
    gates = jax.nn.sigmoid(_dot(hb, wg_ref[...]))
    m = gates[:, :d] * _dot(oa_ref[...], wabr_ref[...]) + gates[:, d:] * _dot(ob_ref[...], wbbr_ref[...])
    h = x + _dot(m.astype(BF16), wout_ref[...])
    hb2 = _rms(h, gffn_ref[...]).astype(BF16)

    @pl.when(j == 0)
    def _():
        carry_ref[...] = cbuf_ref[...]

    acc_ref[...] = h
    for c in range(n_chunks):
        cols = slice(c * 2 * fc, (c + 1) * 2 * fc)
        u = _dot(hb2, wup_ref[:, cols])
        ext_ref[0:SUBLANES, :] = carry_ref[:, cols]
        ext_ref[SUBLANES:SUBLANES + tm, :] = u
        cw = cw_ref[:, cols]
        conv = (cb_ref[:, cols] + cw[0:1] * ext_ref[SUBLANES - 2:SUBLANES - 2 + tm, :]
                + cw[1:2] * ext_ref[SUBLANES - 1:SUBLANES - 1 + tm, :] + cw[2:3] * u)
        carry_ref[:, cols] = ext_ref[tm:tm + SUBLANES, :]
        a_up = conv[:, :fc]
        g_up = conv[:, fc:]
        act = (g_up * jax.nn.sigmoid(g_up) * a_up).astype(BF16)
        acc_ref[...] += _dot(act, wdn_ref[c * fc:(c + 1) * fc, :])
    h_ref[...] = acc_ref[...]

    @pl.when(j == pl.num_programs(1) - 1)
    def _():
        cst_ref[...] = carry_ref[...]


def _post(x, oa, ob, cbuf8, lw, tm):
    b, L, d = x.shape
    ff2 = lw["wup"].shape[1]
    fc = lw["fc"]
    n_chunks = ff2 // (2 * fc)
    row = lambda w: pl.BlockSpec((None, tm, w), lambda i, j: (i, j, 0))
    st = pl.BlockSpec((None, SUBLANES, ff2), lambda i, j: (i, 0, 0))
    weights = [lw["g_mix"], lw["wg"], lw["wabr"], lw["wbbr"], lw["wout"], lw["g_ffn"], lw["wup"],
               lw["conv_w"], lw["conv_b"], lw["wdn"]]
    return pl.pallas_call(
        functools.partial(_post_kernel, tm=tm, fc=fc, n_chunks=n_chunks),
        grid=(b, L // tm),
        in_specs=[row(d), row(oa.shape[-1]), row(ob.shape[-1])] + [_const_spec(w.shape) for w in weights] + [st],
        out_specs=[row(d), st],
        out_shape=[jax.ShapeDtypeStruct((b, L, d), F32), jax.ShapeDtypeStruct((b, SUBLANES, ff2), F32)],
        scratch_shapes=[pltpu.VMEM((tm, d), F32), pltpu.VMEM((tm + SUBLANES, 2 * fc), F32),
                        pltpu.VMEM((SUBLANES, ff2), F32)],
        compiler_params=pltpu.CompilerParams(dimension_semantics=("parallel", "arbitrary"),
                                             vmem_limit_bytes=VMEM_LIMIT),
    )(x, oa, ob, *weights, cbuf8)


def _norm_kernel(x_ref, g_ref, o_ref):
    o_ref[...] = _rms(x_ref[...], g_ref[...])


def _final_norm(x, g, tm):
    b, L, d = x.shape
    row = pl.BlockSpec((None, tm, d), lambda i, j: (i, j, 0))
    return pl.pallas_call(
        _norm_kernel, grid=(b, L // tm), in_specs=[row, _const_spec((1, d))], out_specs=row,
        out_shape=jax.ShapeDtypeStruct((b, L, d), F32),
        compiler_params=pltpu.CompilerParams(dimension_semantics=("parallel", "parallel")),
    )(x, g.reshape(1, d))


def _layer_weights(l, norm_mix_g, w_in, b_q_norm_g, w_b_uq, b_kv_norm_g, w_b_ukv, w_a_br, w_b_br, w_out,
                   norm_ffn_g, w_up, conv_w, conv_b, w_down):
    d = w_in.shape[1]
    da = H_A * 2 * HD_A
    q_lora = w_b_uq.shape[1]
    kvl = w_b_ukv.shape[1]
    d_ff = w_down.shape[1]
    w = w_in[l]
    o_cq = 3 * da
    o_ckv = o_cq + q_lora
    o_kr = o_ckv + kvl
    o_g = o_kr + ROPE_B
    bf = lambda a: a.astype(BF16)
    wv = w[:, 2 * da:3 * da]
    wkr = jnp.zeros((d, LANES), F32).at[:, ROPE_LO:ROPE_LO + ROPE_B].set(w[:, o_kr:o_g])
    wuq = jnp.pad(w_b_uq[l], ((0, 0), (0, 0), (0, HB_PAD - NOPE_B - ROPE_B))).reshape(q_lora, H_B * HB_PAD)
    wuk = jnp.pad(w_b_ukv[l][:, :, :NOPE_B], ((0, 0), (0, 0), (0, HB_PAD - NOPE_B))).reshape(kvl, H_B * HB_PAD)
    wuv = w_b_ukv[l][:, :, NOPE_B:].reshape(kvl, H_B * V_B)
    fc = 256 if d_ff % 256 == 0 else d_ff
    nch = d_ff // fc
    regroup = lambda a: a.reshape(a.shape[0], 2, nch, fc).transpose(0, 2, 1, 3).reshape(a.shape[0], 2 * d_ff)
    return dict(
        g_mix=norm_mix_g[l].reshape(1, d), wq=bf(w[:, :da]), wq_t=bf(w[:, :da].T), wk=bf(w[:, da:2 * da]),
        wv=bf(wv), wv_t=bf(wv.T),
        wcq=bf(w[:, o_cq:o_ckv]), wckv=bf(w[:, o_ckv:o_kr]), wkr=bf(wkr), wg=bf(w[:, o_g:]),
        g_q=b_q_norm_g[l].reshape(1, q_lora), wuq=bf(wuq), wuq_t=bf(wuq.T),
        g_kv=b_kv_norm_g[l].reshape(1, kvl), wuk=bf(wuk),
        wuv=bf(wuv), wuv_t=bf(wuv.T), wabr=bf(w_a_br[l]), wbbr=bf(w_b_br[l]), wout=bf(w_out[l]),
        g_ffn=norm_ffn_g[l].reshape(1, d), wup=bf(regroup(w_up[l])),
        conv_w=jnp.pad(regroup(conv_w[l]), ((0, SUBLANES - CONV_W), (0, 0))),
        conv_b=regroup(conv_b[l].reshape(1, 2 * d_ff)), wdn=bf(w_down[l]), fc=fc, nch=nch, d_ff=d_ff)


def _ungroup_conv_state(cst, lw):
    b = cst.shape[0]
    s = cst[:, SUBLANES - (CONV_W - 1):, :].reshape(b, CONV_W - 1, lw["nch"], 2, lw["fc"])
    return s.transpose(0, 1, 3, 2, 4).reshape(b, CONV_W - 1, 2 * lw["d_ff"])


def _group_conv_state(buf, lw):
    b = buf.shape[0]
    s = buf.reshape(b, CONV_W - 1, 2, lw["nch"], lw["fc"]).transpose(0, 1, 3, 2, 4)
    s = s.reshape(b, CONV_W - 1, 2 * lw["d_ff"])
    return jnp.pad(s, ((0, 0), (SUBLANES - (CONV_W - 1), 0), (0, 0)))


def _rope_tables(pos):
    inv = ROPE_THETA ** (-jnp.arange(ROPE_HALF, dtype=F32) / ROPE_HALF)
    ang = jnp.asarray(pos).astype(F32)[:, None] * inv[None, :]
    c, s = jnp.cos(ang), jnp.sin(ang)
    z = jnp.zeros_like(c)
    n = pos.shape[0]
    tail = jnp.zeros((n, LANES - ROPE_LO - ROPE_B), F32)
    head0 = jnp.zeros((n, ROPE_LO), F32)
    cos = jnp.concatenate([jnp.ones((n, ROPE_LO), F32), c, c, tail], axis=1)
    sin_lo = jnp.concatenate([head0, -s, z, tail], axis=1)
    sin_hi = jnp.concatenate([head0, z, s, tail], axis=1)
    return cos, sin_lo, sin_hi, c.T, s.T


def _pad_rows(a, n, axis):
    pad = [(0, 0)] * a.ndim
    pad[axis] = (0, n - a.shape[axis])
    return jnp.pad(a, pad)


def _feature_major_values(v, heads, dv):
    b, rows, _ = v.shape
    vt = jnp.swapaxes(v, 1, 2).reshape(b, heads, dv, rows)
    extra = jnp.zeros((b, heads, BF16_ROWS, rows), BF16).at[:, :, 0, :].set(1.0)
    return jnp.concatenate([vt, extra], axis=2).reshape(b, heads * (dv + BF16_ROWS), rows)


def _row_tile(L, want):
    t = min(L, want)
    while L % t:
        t //= 2
    return t


def kernel(x_prompt, x_sample, cache_a_k, cache_a_v, cache_b_ckv, cache_b_krope, state_ffn_conv, meta_tokens,
           rel_bias_table, norm_mix_g, w_in, a_lambda, a_subln_g, b_q_norm_g, w_b_uq, b_kv_norm_g, w_b_ukv,
           w_a_br, w_b_br, w_out, norm_ffn_g, w_up, conv_w, conv_b, w_down, final_norm_g):
    bp, seq, d = x_prompt.shape
    bs, dec, _ = x_sample.shape
    depth = w_in.shape[0]
    past = cache_a_k.shape[2]
    da = H_A * 2 * HD_A
    ff2 = w_up.shape[2]
    assert seq % CHUNK == 0 and N_META % SUBLANES == 0 and dec % SUBLANES == 0

    meta_pos = np.arange(-N_META, 0)
    prompt_pos = np.arange(seq)
    sample_pos = np.arange(past, past + dec)
    tabs_m, tabs_p, tabs_s = _rope_tables(meta_pos), _rope_tables(prompt_pos), _rope_tables(sample_pos)

    tm_p = _row_tile(seq, 256)
    tq_p = tk_p = _row_tile(seq, 256)
    qpad = lambda a: _pad_rows(jnp.swapaxes(a, 1, 2), LANES, 2)
    mq_pos = np.concatenate([meta_pos, np.full(LANES - N_META, -1)])
    mk_valid = np.arange(LANES) < N_META
    sq_pos = np.concatenate([sample_pos, np.full(LANES - dec, sample_pos[-1])])
    lk_s = -(-(past + dec) // LANES) * LANES
    sk_pos = np.concatenate([np.arange(past + dec), np.zeros(lk_s - past - dec, np.int64)])
    sk_valid = np.arange(lk_s) < past + dec
    tm_c = _row_tile(past, 256)
    plan_m = _attention_plan(rel_bias_table, mq_pos, mq_pos, mk_valid, LANES, LANES, has_meta=False)
    plan_p = _attention_plan(rel_bias_table, prompt_pos, prompt_pos, np.ones(seq, bool), tq_p, tk_p,
                             has_meta=True)
    plan_s = _attention_plan(rel_bias_table, sq_pos, sk_pos, sk_valid, LANES, lk_s, has_meta=True)

    h_m = meta_tokens[None].astype(F32)
    h_p, h_s = x_prompt, x_sample
    outs = {k: [] for k in ("ak_p", "ak_s", "av_p", "av_s", "ck_p", "ck_s", "kr_p", "kr_s", "cv_p", "cv_s")}
    for l in range(depth):
        lam_init = 0.8 - 0.6 * math.exp(-0.3 * l)
        lw = _layer_weights(l, norm_mix_g, w_in, b_q_norm_g, w_b_uq, b_kv_norm_g, w_b_ukv, w_a_br, w_b_br,
                            w_out, norm_ffn_g, w_up, conv_w, conv_b, w_down)
        attn = functools.partial(_attention, a_lambda=a_lambda[l], subln_g=a_subln_g[l], lam_init=lam_init)

        pm = _projection(h_m, lw, tabs_m, N_META, transposed=False)
        meta = dict(ka=_pad_rows(pm["k16"][0], LANES, 0),
                    vat=_pad_rows(_feature_major_values(pm["v16"], H_A, DV_A)[0], LANES, 1),
                    kb=_pad_rows(pm["kb"][0], LANES, 0),
                    vbt=_pad_rows(_feature_major_values(pm["vb"], H_B, V_B)[0], LANES, 1))
        qm = dict(qa1=qpad(pm["qa1"]), qa2=qpad(pm["qa2"]), qb=qpad(pm["qb"]),
                  ka=meta["ka"][None], vat=meta["vat"][None], kb=meta["kb"][None], vbt=meta["vbt"][None])
        oa, ob = attn(qm, None, plan_m)
        h_m, cst_m = _post(h_m, oa[:, :N_META], ob[:, :N_META], jnp.zeros((1, SUBLANES, ff2), F32), lw, N_META)

        pp = _projection(h_p, lw, tabs_p, tm_p, transposed=True)
        qp = dict(qa1=pp["qa1"], qa2=pp["qa2"], qb=pp["qb"], ka=pp["k16"], vat=pp["v16"], kb=pp["kb"],
                  vbt=pp["vb"])
        oa, ob = attn(qp, meta, plan_p)
        h_p, cst_p = _post(h_p, oa, ob, jnp.broadcast_to(cst_m, (bp, SUBLANES, ff2)), lw, tm_p)

        ps = _projection(h_s, lw, tabs_s, dec, transposed=False)
        kr_cache = jnp.pad(cache_b_krope[l], ((0, 0), (0, 0), (ROPE_LO, LANES - ROPE_LO - ROPE_B)))
        kb_c, vbt_c = _kv_up_cache(cache_b_ckv[l], kr_cache, lw, tm_c)
        va_all = jnp.concatenate([cache_a_v[l].reshape(bs, past, da).astype(BF16), ps["v16"]], axis=1)
        vb_all = jnp.concatenate([jnp.swapaxes(vbt_c, 1, 2), ps["vb"]], axis=1)
        qs = dict(qa1=qpad(ps["qa1"]), qa2=qpad(ps["qa2"]), qb=qpad(ps["qb"]),
                  ka=_pad_rows(jnp.concatenate([cache_a_k[l].reshape(bs, past, da).astype(BF16), ps["k16"]],
                                               axis=1), lk_s, 1),
                  vat=_pad_rows(_feature_major_values(va_all, H_A, DV_A), lk_s, 2),
                  kb=_pad_rows(jnp.concatenate([kb_c, ps["kb"]], axis=1), lk_s, 1),
                  vbt=_pad_rows(_feature_major_values(vb_all, H_B, V_B), lk_s, 2))
        oa, ob = attn(qs, meta, plan_s)
        h_s, cst_s = _post(h_s, oa[:, :dec], ob[:, :dec], _group_conv_state(state_ffn_conv[l], lw), lw, dec)

        bc = lambda a, n: jnp.broadcast_to(a, (n,) + a.shape[1:])
        heads = lambda a: a.reshape(a.shape[0], a.shape[1], H_A, 2 * HD_A)
        outs["ak_p"].append(jnp.concatenate([bc(heads(pm["kf"]), bp), heads(pp["kf"])], axis=1))
        outs["av_p"].append(jnp.concatenate([bc(heads(pm["vf"]), bp), heads(pp["vf"])], axis=1))
        outs["ck_p"].append(jnp.concatenate([bc(pm["ckv"], bp), pp["ckv"]], axis=1))
        outs["kr_p"].append(jnp.concatenate([bc(pm["kr"], bp), pp["kr"]], axis=1))
        outs["cv_p"].append(_ungroup_conv_state(cst_p, lw))
        outs["ak_s"].append(heads(ps["kf"]))
        outs["av_s"].append(heads(ps["vf"]))
        outs["ck_s"].append(ps["ckv"])
        outs["kr_s"].append(ps["kr"])
        outs["cv_s"].append(_ungroup_conv_state(cst_s, lw))

    y_p = _final_norm(h_p, final_norm_g, tm_p)
    y_s = _final_norm(h_s, final_norm_g, dec)
    st = lambda k: jnp.stack(outs[k])
    return (y_p, y_s, st("ak_p"), st("ak_s"), st("av_p"), st("av_s"), st("ck_p"), st("ck_s"),
            st("kr_p"), st("kr_s"), st("cv_p"), st("cv_s"))
```

```python
import functools
import math

import numpy as np
import jax
import jax.numpy as jnp
from jax import lax
from jax.experimental import pallas as pl
from jax.experimental.pallas import tpu as pltpu

CHUNK = 64
N_META = 16
EPS = 1e-6
H_A = 8
HD_A = 64
H_B = 8
NOPE_B = 64
ROPE_B = 32
V_B = 64
ROPE_THETA = 10000.0
NUM_BUCKETS = 32
MAX_DISTANCE = 128
CONV_W = 3

LANES = 128
SUBLANES = 8
BF16_ROWS = 16
LOG2E = 1.4426950408889634
MASKED = -1e30
FAR_REL = 91
VMEM_LIMIT = 56 * 1024 * 1024

F32 = jnp.float32
BF16 = jnp.bfloat16
HB_PAD = LANES
ROPE_LO = NOPE_B
ROPE_HALF = ROPE_B // 2
DV_A = 2 * HD_A
VA_PAD = DV_A + BF16_ROWS
VB_PAD = V_B + BF16_ROWS


def _dot(a, b):
    return jnp.dot(a, b, preferred_element_type=F32)


def _dot_nt(a, b):
    return lax.dot_general(a, b, (((1,), (1,)), ((), ())), preferred_element_type=F32)


def _rms(x, g):
    return x * lax.rsqrt(jnp.mean(x * x, axis=-1, keepdims=True) + EPS) * g


def _rope_padded(x, cos, sin_lo, sin_hi):
    return (x * cos + pltpu.roll(x, LANES - ROPE_HALF, 1) * sin_lo
            + pltpu.roll(x, ROPE_HALF, 1) * sin_hi)


def _const_spec(shape):
    nd = len(shape)
    return pl.BlockSpec(shape, lambda *_: (0,) * nd, pipeline_mode=pl.Buffered(1))


def _sum_rows(n):
    r = lax.broadcasted_iota(jnp.int32, (BF16_ROWS, n), 0)
    return jnp.where(r == 0, 1.0, 0.0).astype(BF16)


N_PROJ_INPUTS = 19


def _proj_kernel(*refs, transposed, n_alias):
    (x_ref, g_ref, wq_ref, wk_ref, wv_ref, wv2_ref, wcq_ref, wckv_ref, wkr_ref, qng_ref,
     wuq_ref, kvng_ref, wuk_ref, wuv_ref, cos_ref, slo_ref, shi_ref, cost_ref, sint_ref) = refs[:N_PROJ_INPUTS]
    (qa1_ref, qa2_ref, kf_ref, k16_ref, vf_ref, v16_ref, qb_ref, ckv_ref, kr_ref, kb_ref,
     vb_ref) = refs[N_PROJ_INPUTS + n_alias:]
    x = x_ref[...]
    rows = x.shape[0]
    hb = _rms(x, g_ref[...]).astype(BF16)
    scale_a = HD_A ** -0.5 * LOG2E
    scale_b = (NOPE_B + ROPE_B) ** -0.5 * LOG2E
    cos = cos_ref[...]
    slo = slo_ref[...]
    shi = shi_ref[...]

    if transposed:
        qa = _dot_nt(wq_ref[...], hb) * scale_a
        feat = lax.broadcasted_iota(jnp.int32, qa.shape, 0) % (2 * HD_A)
    else:
        qa = _dot(hb, wq_ref[...]) * scale_a
        feat = lax.broadcasted_iota(jnp.int32, qa.shape, 1) % (2 * HD_A)
    qa1_ref[...] = jnp.where(feat < HD_A, qa, 0.0).astype(BF16)
    qa2_ref[...] = jnp.where(feat >= HD_A, qa, 0.0).astype(BF16)

    def store_rows(ref, val):
        ref[(0,) * (len(ref.shape) - 2)] = val

    k = _dot(hb, wk_ref[...])
    store_rows(kf_ref, k)
    k16_ref[...] = k.astype(BF16)
    v = _dot(hb, wv_ref[...])
    store_rows(vf_ref, v)
    if transposed:
        vt = _dot_nt(wv2_ref[...], hb).astype(BF16)
        ones = _sum_rows(rows)
        for h in range(H_A):
            v16_ref[h * VA_PAD:h * VA_PAD + DV_A, :] = vt[h * DV_A:(h + 1) * DV_A]
            v16_ref[h * VA_PAD + DV_A:(h + 1) * VA_PAD, :] = ones
    else:
        v16_ref[...] = v.astype(BF16)

    cq = _rms(_dot(hb, wcq_ref[...]), qng_ref[...]).astype(BF16)
    if transposed:
        qb = _dot_nt(wuq_ref[...], cq) * scale_b
        ct = cost_ref[...]
        st = sint_ref[...]
        for h in range(H_B):
            lo = h * HB_PAD + ROPE_LO
            x1 = qb[lo:lo + ROPE_HALF]
            x2 = qb[lo + ROPE_HALF:lo + ROPE_B]
            qb_ref[h * HB_PAD:lo, :] = qb[h * HB_PAD:lo].astype(BF16)
            qb_ref[lo:lo + ROPE_HALF, :] = (x1 * ct - x2 * st).astype(BF16)
            qb_ref[lo + ROPE_HALF:lo + ROPE_B, :] = (x1 * st + x2 * ct).astype(BF16)
            qb_ref[lo + ROPE_B:(h + 1) * HB_PAD, :] = qb[lo + ROPE_B:(h + 1) * HB_PAD].astype(BF16)
    else:
        qb = _dot(cq, wuq_ref[...]) * scale_b
        for h in range(H_B):
            sl = slice(h * HB_PAD, (h + 1) * HB_PAD)
            qb_ref[:, sl] = _rope_padded(qb[:, sl], cos, slo, shi).astype(BF16)

    ckv = _rms(_dot(hb, wckv_ref[...]), kvng_ref[...])
    store_rows(ckv_ref, ckv)
    kr = _rope_padded(_dot(hb, wkr_ref[...]), cos, slo, shi)
    store_rows(kr_ref, kr[:, ROPE_LO:ROPE_LO + ROPE_B])
    cb = ckv.astype(BF16)
    kn = _dot(cb, wuk_ref[...])
    for h in range(H_B):
        sl = slice(h * HB_PAD, (h + 1) * HB_PAD)
        kb_ref[:, sl] = (kn[:, sl] + kr).astype(BF16)
    if transposed:
        vbt = _dot_nt(wuv_ref[...], cb).astype(BF16)
        ones = _sum_rows(rows)
        for h in range(H_B):
            vb_ref[h * VB_PAD:h * VB_PAD + V_B, :] = vbt[h * V_B:(h + 1) * V_B]
            vb_ref[h * VB_PAD + V_B:(h + 1) * VB_PAD, :] = ones
    else:
        vb_ref[...] = _dot(cb, wuv_ref[...]).astype(BF16)


CACHE_NAMES = ("kf", "vf", "ckv", "kr")


def _projection(x, lw, tabs, tm, transposed, cache=None):
    b, L, d = x.shape
    da = lw["wq"].shape[1]
    kvl = lw["wckv"].shape[1]
    hv = H_B * V_B
    hq = H_B * HB_PAD
    nj = L // tm
    row = lambda w: pl.BlockSpec((None, tm, w), lambda i, j: (i, j, 0))
    col = lambda w: pl.BlockSpec((None, w, tm), lambda i, j: (i, 0, j))
    tab = pl.BlockSpec((tm, LANES), lambda i, j: (j, 0))
    tab_t = pl.BlockSpec((ROPE_HALF, tm), lambda i, j: (0, j))
    t = "_t" if transposed else ""
    weights = [lw["g_mix"], lw["wq" + t], lw["wk"], lw["wv"], lw["wv_t"], lw["wcq"], lw["wckv"], lw["wkr"],
               lw["g_q"], lw["wuq" + t], lw["g_kv"], lw["wuk"], lw["wuv" + t]]
    in_specs = [row(d)] + [_const_spec(w.shape) for w in weights] + [tab, tab, tab, tab_t, tab_t]
    sds = jax.ShapeDtypeStruct
    if transposed:
        feat = lambda w, wt: (sds((b, wt, L), BF16), col(wt))
    else:
        feat = lambda w, wt: (sds((b, L, w), BF16), row(w))
    if cache is None:
        f32row = lambda w: (sds((b, L, w), F32), row(w))
    else:
        depth, layer, lead, bufs = cache
        f32row = lambda w: (sds((depth, b, lead + L, w), F32),
                            pl.BlockSpec((pl.Element(1), pl.Element(1), pl.Element(tm), pl.Element(w)),
                                         lambda i, j: (layer, i, pl.multiple_of(lead + j * tm, SUBLANES), 0)))
    outs = [feat(da, da), feat(da, da), f32row(da), (sds((b, L, da), BF16), row(da)),
            f32row(da), feat(da, H_A * VA_PAD), feat(hq, hq), f32row(kvl), f32row(ROPE_B),
            (sds((b, L, hq), BF16), row(hq)), feat(hv, H_B * VB_PAD)]
    names = ("qa1", "qa2", "kf", "k16", "vf", "v16", "qb", "ckv", "kr", "kb", "vb")
    args = [x, *weights, *tabs]
    aliases = {}
    if cache is not None and bufs is not None:
        for name in CACHE_NAMES:
            aliases[len(args)] = names.index(name)
            args.append(bufs[name])
            in_specs.append(pl.BlockSpec(memory_space=pl.ANY))
    res = pl.pallas_call(
        functools.partial(_proj_kernel, transposed=transposed, n_alias=len(aliases)),
        grid=(b, nj), in_specs=in_specs, out_specs=[s for _, s in outs], out_shape=[s for s, _ in outs],
        input_output_aliases=aliases,
        compiler_params=pltpu.CompilerParams(dimension_semantics=("parallel", "parallel"),
                                             vmem_limit_bytes=VMEM_LIMIT),
    )(*args)
    return dict(zip(names, res))


def _kvup_kernel(ckv_ref, kr_ref, wuk_ref, wuvt_ref, kb_ref, vbt_ref):
    cb = ckv_ref[...].astype(BF16)
    kn = _dot(cb, wuk_ref[...])
    kr = kr_ref[...]
    for h in range(H_B):
        sl = slice(h * HB_PAD, (h + 1) * HB_PAD)
        kb_ref[:, sl] = (kn[:, sl] + kr).astype(BF16)
    vbt_ref[...] = _dot_nt(wuvt_ref[...], cb).astype(BF16)


def _kv_up_cache(ckv, kr_pad, lw, tm):
    b, p, kvl = ckv.shape
    hv = H_B * V_B
    row = lambda w: pl.BlockSpec((None, tm, w), lambda i, j: (i, j, 0))
    return pl.pallas_call(
        _kvup_kernel, grid=(b, p // tm),
        in_specs=[row(kvl), row(LANES), _const_spec(lw["wuk"].shape), _const_spec(lw["wuv_t"].shape)],
        out_specs=[row(H_B * HB_PAD), pl.BlockSpec((None, hv, tm), lambda i, j: (i, 0, j))],
        out_shape=[jax.ShapeDtypeStruct((b, p, H_B * HB_PAD), BF16), jax.ShapeDtypeStruct((b, hv, p), BF16)],
        compiler_params=pltpu.CompilerParams(dimension_semantics=("parallel", "parallel"),
                                             vmem_limit_bytes=VMEM_LIMIT),
    )(ckv, kr_pad, lw["wuk"], lw["wuv_t"])


N_TILES = 2 * H_A + H_B
SCORE_LAG = 4
PV_LAG = 2
N_SCORE_SLOTS = SCORE_LAG + 2
N_PROB_SLOTS = PV_LAG + 2
STEP_FIRST, STEP_LAST, STEP_BIASED = 1, 2, 4


def _softmax_tiles(tiles, tk, s_slots, p_slots, m_ref, alpha_ref):
    n = len(tiles)

    def scores(t):
        k, q_t, _, bias, _, _ = tiles[t]
        st = _dot(k(), q_t())
        if bias is not None:
            st = st + bias()
        s_slots[t % N_SCORE_SLOTS][0:tk, :] = st
        row = slice(t, t + 1)
        m_old = m_ref[row, :]
        m_new = jnp.maximum(m_old, jnp.max(st, axis=0, keepdims=True))
        m_ref[row, :] = m_new
        alpha_ref[row, :] = jnp.exp2(m_old - m_new)

    def probs(t):
        p = jnp.exp2(s_slots[t % N_SCORE_SLOTS][0:tk, :] - m_ref[t:t + 1, :])
        p_slots[t % N_PROB_SLOTS][0:tk, :] = p.astype(BF16)

    def values(t):
        _, _, v_t, _, acc_ref, idx = tiles[t]
        pv = _dot(v_t(), p_slots[t % N_PROB_SLOTS][0:tk, :])
        acc_ref[idx] = alpha_ref[t:t + 1, :] * acc_ref[idx] + pv

    for i in range(n + SCORE_LAG + PV_LAG):
        if i < n:
            scores(i)
        if 0 <= i - SCORE_LAG < n:
            probs(i - SCORE_LAG)
        if 0 <= i - SCORE_LAG - PV_LAG < n:
            values(i - SCORE_LAG - PV_LAG)


def _attn_kernel(qi_ref, ki_ref, var_ref, mvar_ref, flag_ref, *refs, has_meta, lam_init, tk):
    del qi_ref, ki_ref
    qa1_ref, qa2_ref, ka_ref, vat_ref, qb_ref, kb_ref, vbt_ref, bias_ref = refs[:8]
    refs = refs[8:]
    if has_meta:
        mka_ref, mvat_ref, mkb_ref, mvbt_ref, mbias_ref = refs[:5]
        refs = refs[5:]
    lam_ref, sg_ref, oa_ref, ob_ref, m_ref, alpha_ref, acca, accb = refs[:8]
    s_slots = refs[8:8 + N_SCORE_SLOTS]
    p_slots = refs[8 + N_SCORE_SLOTS:]

    step = pl.program_id(1)
    flags = flag_ref[step]
    first = (flags & STEP_FIRST) != 0
    last = (flags & STEP_LAST) != 0
    biased = (flags & STEP_BIASED) != 0

    def tiles_of(k_ref, vt_ref, k2_ref, vt2_ref, b_ref, variant):
        tiles = []
        for h in range(H_A):
            sl = slice(h * 2 * HD_A, (h + 1) * 2 * HD_A)
            vsl = slice(h * VA_PAD, (h + 1) * VA_PAD)
            bias = None if b_ref is None else (lambda h=h: b_ref[variant, h])
            for mp, q_ref in enumerate((qa1_ref, qa2_ref)):
                tiles.append((lambda sl=sl: k_ref[:, sl], lambda sl=sl, q_ref=q_ref: q_ref[sl, :],
                              lambda vsl=vsl: vt_ref[vsl, :], bias, acca, 2 * h + mp))
        bias = None if b_ref is None else (lambda: b_ref[variant, H_A])
        for h in range(H_B):
            sl = slice(h * HB_PAD, (h + 1) * HB_PAD)
            vsl = slice(h * VB_PAD, (h + 1) * VB_PAD)
            tiles.append((lambda sl=sl: k2_ref[:, sl], lambda sl=sl: qb_ref[sl, :],
                          lambda vsl=vsl: vt2_ref[vsl, :], bias, accb, h))
        return tiles

    @pl.when(first)
    def _():
        m_ref[...] = jnp.full(m_ref.shape, MASKED, F32)
        acca[...] = jnp.zeros(acca.shape, F32)
        accb[...] = jnp.zeros(accb.shape, F32)
        if has_meta:
            _softmax_tiles(tiles_of(mka_ref, mvat_ref, mkb_ref, mvbt_ref, mbias_ref, mvar_ref[step]), LANES,
                           s_slots, p_slots, m_ref, alpha_ref)

    @pl.when(biased)
    def _():
        _softmax_tiles(tiles_of(ka_ref, vat_ref, kb_ref, vbt_ref, bias_ref, var_ref[step]), tk,
                       s_slots, p_slots, m_ref, alpha_ref)

    @pl.when(jnp.logical_not(biased))
    def _():
        _softmax_tiles(tiles_of(ka_ref, vat_ref, kb_ref, vbt_ref, None, None), tk,
                       s_slots, p_slots, m_ref, alpha_ref)

    @pl.when(last)
    def _():
        al = lam_ref[...]
        lam = (jnp.exp(jnp.sum(al[0:1] * al[1:2], axis=1, keepdims=True))
               - jnp.exp(jnp.sum(al[2:3] * al[3:4], axis=1, keepdims=True)) + lam_init)
        sg = sg_ref[...]

        def normalised(acc_ref, idx, dv):
            a = acc_ref[idx]
            return a[0:dv] / a[dv:dv + 1]

        for h in range(H_A):
            o = normalised(acca, 2 * h, DV_A) - lam * normalised(acca, 2 * h + 1, DV_A)
            y = o * lax.rsqrt(jnp.mean(o * o, axis=0, keepdims=True) + EPS) * sg * (1.0 - lam_init)
            oa_ref[:, h * DV_A:(h + 1) * DV_A] = y.T.astype(BF16)
        for j in range(H_B // 2):
            pair = jnp.concatenate([normalised(accb, 2 * j, V_B), normalised(accb, 2 * j + 1, V_B)],
                                   axis=0)
            ob_ref[:, j * 2 * V_B:(j + 1) * 2 * V_B] = pair.T.astype(BF16)


def _t5_bucket(rel):
    half = NUM_BUCKETS // 2
    max_exact = half // 2
    n = jnp.abs(rel)
    nf = jnp.maximum(n, 1).astype(F32)
    large = max_exact + (jnp.log(nf / max_exact) / math.log(MAX_DISTANCE / max_exact)
                         * (half - max_exact)).astype(jnp.int32)
    large = jnp.minimum(large, half - 1)
    return jnp.where(rel > 0, half, 0) + jnp.where(n < max_exact, n, large)


def _bias_tile(table, q_pos, k_pos, k_valid):
    q_pos = jnp.asarray(q_pos, jnp.int32)
    k_pos = jnp.asarray(k_pos, jnp.int32)
    rel = k_pos[:, None] - q_pos[None, :]
    table = table.astype(F32)
    onehot = (_t5_bucket(rel)[..., None] == jnp.arange(NUM_BUCKETS)).astype(F32)
    shifted = (table - table[NUM_BUCKETS // 2 - 1]) * LOG2E
    bias = jnp.einsum("kqb,bh->hkq", onehot, shifted, precision=lax.Precision.HIGHEST)
    bias = jnp.concatenate([bias, jnp.zeros((1,) + rel.shape, F32)], axis=0)
    visible = ((k_pos[:, None] // CHUNK) <= (q_pos[None, :] // CHUNK)) & jnp.asarray(k_valid)[:, None]
    return jnp.where(visible[None], bias, MASKED)


def _plan_tiles(q_pos, k_pos, k_valid, tq, tk, all_pairs):
    nq, nk = len(q_pos) // tq, len(k_pos) // tk
    pairs, variants, defs = [], {}, []
    for qi in range(nq):
        qp = q_pos[qi * tq:(qi + 1) * tq]
        for ki in range(nk):
            kp = k_pos[ki * tk:(ki + 1) * tk]
            kv = k_valid[ki * tk:(ki + 1) * tk]
            vis = ((kp[:, None] // CHUNK) <= (qp[None, :] // CHUNK)) & kv[:, None]
            if not (all_pairs or vis.any()):
                continue
            rel = kp[:, None] - qp[None, :]
            if vis.all() and rel.max() <= -FAR_REL and not all_pairs:
                pairs.append((qi, ki, -1))
                continue
            key = (int(kp[0] - qp[0]), int(qp[0] % CHUNK), int(kp[0] % CHUNK), int(kv.sum()),
                   tuple(np.diff(qp).tolist()) if np.any(np.diff(qp) != 1) else ())
            if key not in variants:
                variants[key] = len(defs)
                defs.append((qp, kp, kv))
            pairs.append((qi, ki, variants[key]))
    return pairs, defs


def _attention_plan(table, q_pos, k_pos, k_valid, tq, tk, has_meta):
    pairs, defs = _plan_tiles(q_pos, k_pos, k_valid, tq, tk, all_pairs=False)
    if not defs:
        defs = [(q_pos[:tq], k_pos[:tk], k_valid[:tk])]
    bias = jnp.stack([_bias_tile(table, *d) for d in defs])
    qi = np.array([a for a, _, _ in pairs], np.int32)
    ki = np.array([a for _, a, _ in pairs], np.int32)
    var = np.array([a for _, _, a in pairs], np.int32)
    flags = np.zeros(len(pairs), np.int32)
    for s in range(len(pairs)):
        if s == 0 or qi[s] != qi[s - 1]:
            flags[s] |= STEP_FIRST
        if s == len(pairs) - 1 or qi[s] != qi[s + 1]:
            flags[s] |= STEP_LAST
        if var[s] >= 0:
            flags[s] |= STEP_BIASED
    assert sorted(set(qi.tolist())) == list(range(len(q_pos) // tq))
    plan = dict(tq=tq, tk=tk, qi=qi, ki=ki, var=np.maximum(var, 0), flags=flags, bias=bias,
                mvar=np.zeros(len(pairs), np.int32), mbias=None)
    if has_meta:
        m_pos = np.concatenate([np.arange(-N_META, 0), np.zeros(LANES - N_META, np.int64)])
        m_valid = np.arange(LANES) < N_META
        mpairs, mdefs = _plan_tiles(q_pos, m_pos, m_valid, tq, LANES, all_pairs=True)
        plan["mbias"] = jnp.stack([_bias_tile(table, *d) for d in mdefs])
        plan["mvar"] = np.array([a for _, _, a in mpairs], np.int32)[qi]
    return plan


def _attention(p, meta, plan, a_lambda, subln_g, lam_init):
    b, da, lq = p["qa1"].shape
    hv = H_B * V_B
    tq, tk = plan["tq"], plan["tk"]
    n_steps = len(plan["qi"])
    has_meta = meta is not None

    qblk = lambda w: pl.BlockSpec((None, tq, w), lambda i, s, qi_, ki_, v_, mv_, f_: (i, qi_[s], 0))
    qtblk = lambda w: pl.BlockSpec((None, w, tq), lambda i, s, qi_, ki_, v_, mv_, f_: (i, 0, qi_[s]))
    kblk = lambda w: pl.BlockSpec((None, tk, w), lambda i, s, qi_, ki_, v_, mv_, f_: (i, ki_[s], 0))
    ktblk = lambda w: pl.BlockSpec((None, w, tk), lambda i, s, qi_, ki_, v_, mv_, f_: (i, 0, ki_[s]))
    slot_rows = max(tk, LANES)
    in_specs = [qtblk(da), qtblk(da), kblk(da), ktblk(H_A * VA_PAD), qtblk(H_B * HB_PAD), kblk(H_B * HB_PAD),
                ktblk(H_B * VB_PAD), _const_spec(plan["bias"].shape)]
    args = [p["qa1"], p["qa2"], p["ka"], p["vat"], p["qb"], p["kb"], p["vbt"], plan["bias"]]
    if has_meta:
        margs = [meta["ka"], meta["vat"], meta["kb"], meta["vbt"], plan["mbias"]]
        in_specs += [_const_spec(a.shape) for a in margs]
        args += margs
    in_specs += [_const_spec(a_lambda.shape), _const_spec((DV_A, 1))]
    args += [a_lambda, subln_g.reshape(DV_A, 1)]
    grid_spec = pltpu.PrefetchScalarGridSpec(
        num_scalar_prefetch=5, grid=(b, n_steps), in_specs=in_specs,
        out_specs=[qblk(da), qblk(hv)],
        scratch_shapes=[pltpu.VMEM((N_TILES, tq), F32), pltpu.VMEM((N_TILES, tq), F32),
                        pltpu.VMEM((2 * H_A, VA_PAD, tq), F32), pltpu.VMEM((H_B, VB_PAD, tq), F32)]
        + [pltpu.VMEM((slot_rows, tq), F32)] * N_SCORE_SLOTS
        + [pltpu.VMEM((slot_rows, tq), BF16)] * N_PROB_SLOTS)
    oa, ob = pl.pallas_call(
        functools.partial(_attn_kernel, has_meta=has_meta, lam_init=lam_init, tk=tk),
        grid_spec=grid_spec,
        out_shape=[jax.ShapeDtypeStruct((b, lq, da), BF16), jax.ShapeDtypeStruct((b, lq, hv), BF16)],
        compiler_params=pltpu.CompilerParams(dimension_semantics=("parallel", "arbitrary"),
                                             vmem_limit_bytes=VMEM_LIMIT),
    )(*(jnp.asarray(plan[k]) for k in ("qi", "ki", "var", "mvar", "flags")), *args)
    return oa, ob


def _post_kernel(x_ref, oa_ref, ob_ref, gmix_ref, wg_ref, wabr_ref, wbbr_ref, wout_ref, gffn_ref, wup_ref,
                 cw_ref, cb_ref, wdn_ref, cbuf_ref, h_ref, cst_ref, acc_ref, ext0_ref, ext1_ref, carry_ref,
                 *, tm, fc, n_chunks):
    j = pl.program_id(1)
    d = x_ref.shape[-1]
    x = x_ref[...]
    hb = _rms(x, gmix_ref[...]).astype(BF16)
    gates = jax.nn.sigmoid(_dot(hb, wg_ref[...]))
    m = gates[:, :d] * _dot(oa_ref[...], wabr_ref[...]) + gates[:, d:] * _dot(ob_ref[...], wbbr_ref[...])
    h = x + _dot(m.astype(BF16), wout_ref[...])
    hb2 = _rms(h, gffn_ref[...]).astype(BF16)

    @pl.when(j == 0)
    def _():
        carry_ref[...] = cbuf_ref[...]

    acc_ref[...] = h
    ext_refs = (ext0_ref, ext1_ref)
    chunk_cols = lambda c: slice(c * 2 * fc, (c + 1) * 2 * fc)

    def up_project(c):
        ext_refs[c % 2][SUBLANES:SUBLANES + tm, :] = _dot(hb2, wup_ref[:, chunk_cols(c)])

    up_project(0)
    for c in range(n_chunks):
        cols = chunk_cols(c)
        ext_ref = ext_refs[c % 2]
        if c + 1 < n_chunks:
            up_project(c + 1)
        ext_ref[0:SUBLANES, :] = carry_ref[:, cols]
        u = ext_ref[SUBLANES:SUBLANES + tm, :]
        cw = cw_ref[:, cols]
        conv = (cb_ref[:, cols] + cw[0:1] * ext_ref[SUBLANES - 2:SUBLANES - 2 + tm, :]
                + cw[1:2] * ext_ref[SUBLANES - 1:SUBLANES - 1 + tm, :] + cw[2:3] * u)
        carry_ref[:, cols] = ext_ref[tm:tm + SUBLANES, :]
        a_up = conv[:, :fc]
        g_up = conv[:, fc:]
        act = (g_up * jax.nn.sigmoid(g_up) * a_up).astype(BF16)
        acc_ref[...] += _dot(act, wdn_ref[c * fc:(c + 1) * fc, :])
    h_ref[...] = acc_ref[...]

    @pl.when(j == pl.num_programs(1) - 1)
    def _():
        cst_ref[...] = carry_ref[...]


def _post(x, oa, ob, cbuf8, lw, tm):
    b, L, d = x.shape
    ff2 = lw["wup"].shape[1]
    fc = lw["fc"]
    n_chunks = ff2 // (2 * fc)
    row = lambda w: pl.BlockSpec((None, tm, w), lambda i, j: (i, j, 0))
    st = pl.BlockSpec((None, SUBLANES, ff2), lambda i, j: (i, 0, 0))
    weights = [lw["g_mix"], lw["wg"], lw["wabr"], lw["wbbr"], lw["wout"], lw["g_ffn"], lw["wup"],
               lw["conv_w"], lw["conv_b"], lw["wdn"]]
    return pl.pallas_call(
        functools.partial(_post_kernel, tm=tm, fc=fc, n_chunks=n_chunks),
        grid=(b, L // tm),
        in_specs=[row(d), row(oa.shape[-1]), row(ob.shape[-1])] + [_const_spec(w.shape) for w in weights] + [st],
        out_specs=[row(d), st],
        out_shape=[jax.ShapeDtypeStruct((b, L, d), F32), jax.ShapeDtypeStruct((b, SUBLANES, ff2), F32)],
        scratch_shapes=[pltpu.VMEM((tm, d), F32), pltpu.VMEM((tm + SUBLANES, 2 * fc), F32),
                        pltpu.VMEM((tm + SUBLANES, 2 * fc), F32), pltpu.VMEM((SUBLANES, ff2), F32)],
        compiler_params=pltpu.CompilerParams(dimension_semantics=("parallel", "arbitrary"),
                                             vmem_limit_bytes=VMEM_LIMIT),
    )(x, oa, ob, *weights, cbuf8)


def _norm_kernel(x_ref, g_ref, o_ref):
    o_ref[...] = _rms(x_ref[...], g_ref[...])


def _final_norm(x, g, tm):
    b, L, d = x.shape
    row = pl.BlockSpec((None, tm, d), lambda i, j: (i, j, 0))
    return pl.pallas_call(
        _norm_kernel, grid=(b, L // tm), in_specs=[row, _const_spec((1, d))], out_specs=row,
        out_shape=jax.ShapeDtypeStruct((b, L, d), F32),
        compiler_params=pltpu.CompilerParams(dimension_semantics=("parallel", "parallel")),
    )(x, g.reshape(1, d))


def _layer_weights(l, norm_mix_g, w_in, b_q_norm_g, w_b_uq, b_kv_norm_g, w_b_ukv, w_a_br, w_b_br, w_out,
                   norm_ffn_g, w_up, conv_w, conv_b, w_down):
    d = w_in.shape[1]
    da = H_A * 2 * HD_A
    q_lora = w_b_uq.shape[1]
    kvl = w_b_ukv.shape[1]
    d_ff = w_down.shape[1]
    w = w_in[l]
    o_cq = 3 * da
    o_ckv = o_cq + q_lora
    o_kr = o_ckv + kvl
    o_g = o_kr + ROPE_B
    bf = lambda a: a.astype(BF16)
    wv = w[:, 2 * da:3 * da]
    wkr = jnp.zeros((d, LANES), F32).at[:, ROPE_LO:ROPE_LO + ROPE_B].set(w[:, o_kr:o_g])
    wuq = jnp.pad(w_b_uq[l], ((0, 0), (0, 0), (0, HB_PAD - NOPE_B - ROPE_B))).reshape(q_lora, H_B * HB_PAD)
    wuk = jnp.pad(w_b_ukv[l][:, :, :NOPE_B], ((0, 0), (0, 0), (0, HB_PAD - NOPE_B))).reshape(kvl, H_B * HB_PAD)
    wuv = w_b_ukv[l][:, :, NOPE_B:].reshape(kvl, H_B * V_B)
    fc = 256 if d_ff % 256 == 0 else d_ff
    nch = d_ff // fc
    regroup = lambda a: a.reshape(a.shape[0], 2, nch, fc).transpose(0, 2, 1, 3).reshape(a.shape[0], 2 * d_ff)
    return dict(
        g_mix=norm_mix_g[l].reshape(1, d), wq=bf(w[:, :da]), wq_t=bf(w[:, :da].T), wk=bf(w[:, da:2 * da]),
        wv=bf(wv), wv_t=bf(wv.T),
        wcq=bf(w[:, o_cq:o_ckv]), wckv=bf(w[:, o_ckv:o_kr]), wkr=bf(wkr), wg=bf(w[:, o_g:]),
        g_q=b_q_norm_g[l].reshape(1, q_lora), wuq=bf(wuq), wuq_t=bf(wuq.T),
        g_kv=b_kv_norm_g[l].reshape(1, kvl), wuk=bf(wuk),
        wuv=bf(wuv), wuv_t=bf(wuv.T), wabr=bf(w_a_br[l]), wbbr=bf(w_b_br[l]), wout=bf(w_out[l]),
        g_ffn=norm_ffn_g[l].reshape(1, d), wup=bf(regroup(w_up[l])),
        conv_w=jnp.pad(regroup(conv_w[l]), ((0, SUBLANES - CONV_W), (0, 0))),
        conv_b=regroup(conv_b[l].reshape(1, 2 * d_ff)), wdn=bf(w_down[l]), fc=fc, nch=nch, d_ff=d_ff)


def _ungroup_conv_state(cst, lw):
    b = cst.shape[0]
    s = cst[:, SUBLANES - (CONV_W - 1):, :].reshape(b, CONV_W - 1, lw["nch"], 2, lw["fc"])
    return s.transpose(0, 1, 3, 2, 4).reshape(b, CONV_W - 1, 2 * lw["d_ff"])


def _group_conv_state(buf, lw):
    b = buf.shape[0]
    s = buf.reshape(b, CONV_W - 1, 2, lw["nch"], lw["fc"]).transpose(0, 1, 3, 2, 4)
    s = s.reshape(b, CONV_W - 1, 2 * lw["d_ff"])
    return jnp.pad(s, ((0, 0), (SUBLANES - (CONV_W - 1), 0), (0, 0)))


def _rope_tables(pos):
    inv = ROPE_THETA ** (-jnp.arange(ROPE_HALF, dtype=F32) / ROPE_HALF)
    ang = jnp.asarray(pos).astype(F32)[:, None] * inv[None, :]
    c, s = jnp.cos(ang), jnp.sin(ang)
    z = jnp.zeros_like(c)
    n = pos.shape[0]
    tail = jnp.zeros((n, LANES - ROPE_LO - ROPE_B), F32)
    head0 = jnp.zeros((n, ROPE_LO), F32)
    cos = jnp.concatenate([jnp.ones((n, ROPE_LO), F32), c, c, tail], axis=1)
    sin_lo = jnp.concatenate([head0, -s, z, tail], axis=1)
    sin_hi = jnp.concatenate([head0, z, s, tail], axis=1)
    return cos, sin_lo, sin_hi, c.T, s.T


def _pad_rows(a, n, axis):
    pad = [(0, 0)] * a.ndim
    pad[axis] = (0, n - a.shape[axis])
    return jnp.pad(a, pad)


def _feature_major_values(v, heads, dv):
    b, rows, _ = v.shape
    vt = jnp.swapaxes(v, 1, 2).reshape(b, heads, dv, rows)
    extra = jnp.zeros((b, heads, BF16_ROWS, rows), BF16).at[:, :, 0, :].set(1.0)
    return jnp.concatenate([vt, extra], axis=2).reshape(b, heads * (dv + BF16_ROWS), rows)


def _row_tile(L, want):
    t = min(L, want)
    while L % t:
        t //= 2
    return t


def kernel(x_prompt, x_sample, cache_a_k, cache_a_v, cache_b_ckv, cache_b_krope, state_ffn_conv, meta_tokens,
           rel_bias_table, norm_mix_g, w_in, a_lambda, a_subln_g, b_q_norm_g, w_b_uq, b_kv_norm_g, w_b_ukv,
           w_a_br, w_b_br, w_out, norm_ffn_g, w_up, conv_w, conv_b, w_down, final_norm_g):
    bp, seq, d = x_prompt.shape
    bs, dec, _ = x_sample.shape
    depth = w_in.shape[0]
    past = cache_a_k.shape[2]
    da = H_A * 2 * HD_A
    ff2 = w_up.shape[2]
    assert seq % CHUNK == 0 and N_META % SUBLANES == 0 and dec % SUBLANES == 0

    meta_pos = np.arange(-N_META, 0)
    prompt_pos = np.arange(seq)
    sample_pos = np.arange(past, past + dec)
    tabs_m, tabs_p, tabs_s = _rope_tables(meta_pos), _rope_tables(prompt_pos), _rope_tables(sample_pos)

    tm_p = _row_tile(seq, 256)
    tq_p = tk_p = _row_tile(seq, 256)
    qpad = lambda a: _pad_rows(jnp.swapaxes(a, 1, 2), LANES, 2)
    mq_pos = np.concatenate([meta_pos, np.full(LANES - N_META, -1)])
    mk_valid = np.arange(LANES) < N_META
    sq_pos = np.concatenate([sample_pos, np.full(LANES - dec, sample_pos[-1])])
    lk_s = -(-(past + dec) // LANES) * LANES
    sk_pos = np.concatenate([np.arange(past + dec), np.zeros(lk_s - past - dec, np.int64)])
    sk_valid = np.arange(lk_s) < past + dec
    tm_c = _row_tile(past, 256)
    plan_m = _attention_plan(rel_bias_table, mq_pos, mq_pos, mk_valid, LANES, LANES, has_meta=False)
    plan_p = _attention_plan(rel_bias_table, prompt_pos, prompt_pos, np.ones(seq, bool), tq_p, tk_p,
                             has_meta=True)
    plan_s = _attention_plan(rel_bias_table, sq_pos, sk_pos, sk_valid, LANES, lk_s, has_meta=True)

    h_m = meta_tokens[None].astype(F32)
    h_p, h_s = x_prompt, x_sample
    outs = {k: [] for k in ("ak_s", "av_s", "ck_s", "kr_s", "cv_p", "cv_s")}
    meta_rows = {name: [] for name in CACHE_NAMES}
    cache_p = None
    for l in range(depth):
        lam_init = 0.8 - 0.6 * math.exp(-0.3 * l)
        lw = _layer_weights(l, norm_mix_g, w_in, b_q_norm_g, w_b_uq, b_kv_norm_g, w_b_ukv, w_a_br, w_b_br,
                            w_out, norm_ffn_g, w_up, conv_w, conv_b, w_down)
        attn = functools.partial(_attention, a_lambda=a_lambda[l], subln_g=a_subln_g[l], lam_init=lam_init)

        pm = _projection(h_m, lw, tabs_m, N_META, transposed=False)
        meta = dict(ka=_pad_rows(pm["k16"][0], LANES, 0),
                    vat=_pad_rows(_feature_major_values(pm["v16"], H_A, DV_A)[0], LANES, 1),
                    kb=_pad_rows(pm["kb"][0], LANES, 0),
                    vbt=_pad_rows(_feature_major_values(pm["vb"], H_B, V_B)[0], LANES, 1))
        qm = dict(qa1=qpad(pm["qa1"]), qa2=qpad(pm["qa2"]), qb=qpad(pm["qb"]),
                  ka=meta["ka"][None], vat=meta["vat"][None], kb=meta["kb"][None], vbt=meta["vbt"][None])
        oa, ob = attn(qm, None, plan_m)
        h_m, cst_m = _post(h_m, oa[:, :N_META], ob[:, :N_META], jnp.zeros((1, SUBLANES, ff2), F32), lw, N_META)

        pp = _projection(h_p, lw, tabs_p, tm_p, transposed=True, cache=(depth, l, N_META, cache_p))
        cache_p = {name: pp[name] for name in CACHE_NAMES}
        qp = dict(qa1=pp["qa1"], qa2=pp["qa2"], qb=pp["qb"], ka=pp["k16"], vat=pp["v16"], kb=pp["kb"],
                  vbt=pp["vb"])
        oa, ob = attn(qp, meta, plan_p)
        h_p, cst_p = _post(h_p, oa, ob, jnp.broadcast_to(cst_m, (bp, SUBLANES, ff2)), lw, tm_p)

        ps = _projection(h_s, lw, tabs_s, dec, transposed=False)
        kr_cache = jnp.pad(cache_b_krope[l], ((0, 0), (0, 0), (ROPE_LO, LANES - ROPE_LO - ROPE_B)))
        kb_c, vbt_c = _kv_up_cache(cache_b_ckv[l], kr_cache, lw, tm_c)
        va_all = jnp.concatenate([cache_a_v[l].reshape(bs, past, da).astype(BF16), ps["v16"]], axis=1)
        vb_all = jnp.concatenate([jnp.swapaxes(vbt_c, 1, 2), ps["vb"]], axis=1)
        qs = dict(qa1=qpad(ps["qa1"]), qa2=qpad(ps["qa2"]), qb=qpad(ps["qb"]),
                  ka=_pad_rows(jnp.concatenate([cache_a_k[l].reshape(bs, past, da).astype(BF16), ps["k16"]],
                                               axis=1), lk_s, 1),
                  vat=_pad_rows(_feature_major_values(va_all, H_A, DV_A), lk_s, 2),
                  kb=_pad_rows(jnp.concatenate([kb_c, ps["kb"]], axis=1), lk_s, 1),
                  vbt=_pad_rows(_feature_major_values(vb_all, H_B, V_B), lk_s, 2))
        oa, ob = attn(qs, meta, plan_s)
        h_s, cst_s = _post(h_s, oa[:, :dec], ob[:, :dec], _group_conv_state(state_ffn_conv[l], lw), lw, dec)

        heads = lambda a: a.reshape(a.shape[:-1] + (H_A, 2 * HD_A))
        for name in CACHE_NAMES:
            meta_rows[name].append(pm[name])
        outs["cv_p"].append(_ungroup_conv_state(cst_p, lw))
        outs["ak_s"].append(heads(ps["kf"]))
        outs["av_s"].append(heads(ps["vf"]))
        outs["ck_s"].append(ps["ckv"])
        outs["kr_s"].append(ps["kr"])
        outs["cv_s"].append(_ungroup_conv_state(cst_s, lw))

    y_p = _final_norm(h_p, final_norm_g, tm_p)
    y_s = _final_norm(h_s, final_norm_g, dec)
    st = lambda k: jnp.stack(outs[k])

    def with_meta_rows(name):
        rows = jnp.stack(meta_rows[name])
        rows = jnp.broadcast_to(rows, (depth, bp) + rows.shape[2:])
        return cache_p[name].at[:, :, :N_META].set(rows)

    return (y_p, y_s, heads(with_meta_rows("kf")), st("ak_s"), heads(with_meta_rows("vf")), st("av_s"),
            with_meta_rows("ckv"), st("ck_s"), with_meta_rows("kr"), st("kr_s"), st("cv_p"), st("cv_s"))
```

```python
import functools
import math

import numpy as np
import jax
import jax.numpy as jnp
from jax import lax
from jax.experimental import pallas as pl
from jax.experimental.pallas import tpu as pltpu

CHUNK = 64
N_META = 16
EPS = 1e-6
H_A = 8
HD_A = 64
H_B = 8
NOPE_B = 64
ROPE_B = 32
V_B = 64
ROPE_THETA = 10000.0
NUM_BUCKETS = 32
MAX_DISTANCE = 128
CONV_W = 3

LANES = 128
SUBLANES = 8
BF16_ROWS = 16
LOG2E = 1.4426950408889634
MASKED = -1e30
FAR_REL = 91
VMEM_LIMIT = 56 * 1024 * 1024

F32 = jnp.float32
BF16 = jnp.bfloat16
HB_PAD = LANES
ROPE_LO = NOPE_B
ROPE_HALF = ROPE_B // 2
DV_A = 2 * HD_A
VA_PAD = DV_A + BF16_ROWS
VB_PAD = V_B + BF16_ROWS


def _dot(a, b):
    return jnp.dot(a, b, preferred_element_type=F32)


def _dot_nt(a, b):
    return lax.dot_general(a, b, (((1,), (1,)), ((), ())), preferred_element_type=F32)


def _rms(x, g):
    return x * lax.rsqrt(jnp.mean(x * x, axis=-1, keepdims=True) + EPS) * g


def _rope_padded(x, cos, sin_lo, sin_hi):
    return (x * cos + pltpu.roll(x, LANES - ROPE_HALF, 1) * sin_lo
            + pltpu.roll(x, ROPE_HALF, 1) * sin_hi)


def _const_spec(shape):
    nd = len(shape)
    return pl.BlockSpec(shape, lambda *_: (0,) * nd, pipeline_mode=pl.Buffered(1))


def _sum_rows(n):
    r = lax.broadcasted_iota(jnp.int32, (BF16_ROWS, n), 0)
    return jnp.where(r == 0, 1.0, 0.0).astype(BF16)


N_PROJ_INPUTS = 19


def _proj_kernel(*refs, transposed, n_alias):
    (x_ref, g_ref, wq_ref, wk_ref, wv_ref, wv2_ref, wcq_ref, wckv_ref, wkr_ref, qng_ref,
     wuq_ref, kvng_ref, wuk_ref, wuv_ref, cos_ref, slo_ref, shi_ref, cost_ref, sint_ref) = refs[:N_PROJ_INPUTS]
    (qa1_ref, qa2_ref, kf_ref, k16_ref, vf_ref, v16_ref, qb_ref, ckv_ref, kr_ref, kb_ref,
     vb_ref) = refs[N_PROJ_INPUTS + n_alias:]
    x = x_ref[...]
    rows = x.shape[0]
    hb = _rms(x, g_ref[...]).astype(BF16)
    scale_a = HD_A ** -0.5 * LOG2E
    scale_b = (NOPE_B + ROPE_B) ** -0.5 * LOG2E
    cos = cos_ref[...]
    slo = slo_ref[...]
    shi = shi_ref[...]

    if transposed:
        qa = _dot_nt(wq_ref[...], hb) * scale_a
        feat = lax.broadcasted_iota(jnp.int32, qa.shape, 0) % (2 * HD_A)
    else:
        qa = _dot(hb, wq_ref[...]) * scale_a
        feat = lax.broadcasted_iota(jnp.int32, qa.shape, 1) % (2 * HD_A)
    qa1_ref[...] = jnp.where(feat < HD_A, qa, 0.0).astype(BF16)
    qa2_ref[...] = jnp.where(feat >= HD_A, qa, 0.0).astype(BF16)

    def store_rows(ref, val):
        ref[(0,) * (len(ref.shape) - 2)] = val

    k = _dot(hb, wk_ref[...])
    store_rows(kf_ref, k)
    k16_ref[...] = k.astype(BF16)
    v = _dot(hb, wv_ref[...])
    store_rows(vf_ref, v)
    if transposed:
        vt = _dot_nt(wv2_ref[...], hb).astype(BF16)
        ones = _sum_rows(rows)
        for h in range(H_A):
            v16_ref[h * VA_PAD:h * VA_PAD + DV_A, :] = vt[h * DV_A:(h + 1) * DV_A]
            v16_ref[h * VA_PAD + DV_A:(h + 1) * VA_PAD, :] = ones
    else:
        v16_ref[...] = v.astype(BF16)

    cq = _rms(_dot(hb, wcq_ref[...]), qng_ref[...]).astype(BF16)
    if transposed:
        qb = _dot_nt(wuq_ref[...], cq) * scale_b
        ct = cost_ref[...]
        st = sint_ref[...]
        for h in range(H_B):
            lo = h * HB_PAD + ROPE_LO
            x1 = qb[lo:lo + ROPE_HALF]
            x2 = qb[lo + ROPE_HALF:lo + ROPE_B]
            qb_ref[h * HB_PAD:lo, :] = qb[h * HB_PAD:lo].astype(BF16)
            qb_ref[lo:lo + ROPE_HALF, :] = (x1 * ct - x2 * st).astype(BF16)
            qb_ref[lo + ROPE_HALF:lo + ROPE_B, :] = (x1 * st + x2 * ct).astype(BF16)
            qb_ref[lo + ROPE_B:(h + 1) * HB_PAD, :] = qb[lo + ROPE_B:(h + 1) * HB_PAD].astype(BF16)
    else:
        qb = _dot(cq, wuq_ref[...]) * scale_b
        for h in range(H_B):
            sl = slice(h * HB_PAD, (h + 1) * HB_PAD)
            qb_ref[:, sl] = _rope_padded(qb[:, sl], cos, slo, shi).astype(BF16)

    ckv = _rms(_dot(hb, wckv_ref[...]), kvng_ref[...])
    store_rows(ckv_ref, ckv)
    kr = _rope_padded(_dot(hb, wkr_ref[...]), cos, slo, shi)
    store_rows(kr_ref, kr[:, ROPE_LO:ROPE_LO + ROPE_B])
    cb = ckv.astype(BF16)
    kn = _dot(cb, wuk_ref[...])
    for h in range(H_B):
        sl = slice(h * HB_PAD, (h + 1) * HB_PAD)
        kb_ref[:, sl] = (kn[:, sl] + kr).astype(BF16)
    if transposed:
        vbt = _dot_nt(wuv_ref[...], cb).astype(BF16)
        ones = _sum_rows(rows)
        for h in range(H_B):
            vb_ref[h * VB_PAD:h * VB_PAD + V_B, :] = vbt[h * V_B:(h + 1) * V_B]
            vb_ref[h * VB_PAD + V_B:(h + 1) * VB_PAD, :] = ones
    else:
        vb_ref[...] = _dot(cb, wuv_ref[...]).astype(BF16)


CACHE_NAMES = ("kf", "vf", "ckv", "kr")


def _projection(x, lw, tabs, tm, transposed, cache=None):
    b, L, d = x.shape
    da = lw["wq"].shape[1]
    kvl = lw["wckv"].shape[1]
    hv = H_B * V_B
    hq = H_B * HB_PAD
    nj = L // tm
    row = lambda w: pl.BlockSpec((None, tm, w), lambda i, j: (i, j, 0))
    col = lambda w: pl.BlockSpec((None, w, tm), lambda i, j: (i, 0, j))
    tab = pl.BlockSpec((tm, LANES), lambda i, j: (j, 0))
    tab_t = pl.BlockSpec((ROPE_HALF, tm), lambda i, j: (0, j))
    t = "_t" if transposed else ""
    weights = [lw["g_mix"], lw["wq" + t], lw["wk"], lw["wv"], lw["wv_t"], lw["wcq"], lw["wckv"], lw["wkr"],
               lw["g_q"], lw["wuq" + t], lw["g_kv"], lw["wuk"], lw["wuv" + t]]
    in_specs = [row(d)] + [_const_spec(w.shape) for w in weights] + [tab, tab, tab, tab_t, tab_t]
    sds = jax.ShapeDtypeStruct
    if transposed:
        feat = lambda w, wt: (sds((b, wt, L), BF16), col(wt))
    else:
        feat = lambda w, wt: (sds((b, L, w), BF16), row(w))
    if cache is None:
        f32row = lambda w: (sds((b, L, w), F32), row(w))
    else:
        depth, layer, lead, bufs = cache
        f32row = lambda w: (sds((depth, b, lead + L, w), F32),
                            pl.BlockSpec((pl.Element(1), pl.Element(1), pl.Element(tm), pl.Element(w)),
                                         lambda i, j: (layer, i, pl.multiple_of(lead + j * tm, SUBLANES), 0)))
    outs = [feat(da, da), feat(da, da), f32row(da), (sds((b, L, da), BF16), row(da)),
            f32row(da), feat(da, H_A * VA_PAD), feat(hq, hq), f32row(kvl), f32row(ROPE_B),
            (sds((b, L, hq), BF16), row(hq)), feat(hv, H_B * VB_PAD)]
    names = ("qa1", "qa2", "kf", "k16", "vf", "v16", "qb", "ckv", "kr", "kb", "vb")
    args = [x, *weights, *tabs]
    aliases = {}
    if cache is not None and bufs is not None:
        for name in CACHE_NAMES:
            aliases[len(args)] = names.index(name)
            args.append(bufs[name])
            in_specs.append(pl.BlockSpec(memory_space=pl.ANY))
    res = pl.pallas_call(
        functools.partial(_proj_kernel, transposed=transposed, n_alias=len(aliases)),
        grid=(b, nj), in_specs=in_specs, out_specs=[s for _, s in outs], out_shape=[s for s, _ in outs],
        input_output_aliases=aliases,
        compiler_params=pltpu.CompilerParams(dimension_semantics=("parallel", "parallel"),
                                             vmem_limit_bytes=VMEM_LIMIT),
    )(*args)
    return dict(zip(names, res))


def _kvup_kernel(ckv_ref, kr_ref, wuk_ref, wuvt_ref, kb_ref, vbt_ref):
    cb = ckv_ref[...].astype(BF16)
    kn = _dot(cb, wuk_ref[...])
    kr = kr_ref[...]
    for h in range(H_B):
        sl = slice(h * HB_PAD, (h + 1) * HB_PAD)
        kb_ref[:, sl] = (kn[:, sl] + kr).astype(BF16)
    vbt_ref[...] = _dot_nt(wuvt_ref[...], cb).astype(BF16)


def _kv_up_cache(ckv, kr_pad, lw, tm):
    b, p, kvl = ckv.shape
    hv = H_B * V_B
    row = lambda w: pl.BlockSpec((None, tm, w), lambda i, j: (i, j, 0))
    return pl.pallas_call(
        _kvup_kernel, grid=(b, p // tm),
        in_specs=[row(kvl), row(LANES), _const_spec(lw["wuk"].shape), _const_spec(lw["wuv_t"].shape)],
        out_specs=[row(H_B * HB_PAD), pl.BlockSpec((None, hv, tm), lambda i, j: (i, 0, j))],
        out_shape=[jax.ShapeDtypeStruct((b, p, H_B * HB_PAD), BF16), jax.ShapeDtypeStruct((b, hv, p), BF16)],
        compiler_params=pltpu.CompilerParams(dimension_semantics=("parallel", "parallel"),
                                             vmem_limit_bytes=VMEM_LIMIT),
    )(ckv, kr_pad, lw["wuk"], lw["wuv_t"])


N_TILES = 2 * H_A + H_B
SCORE_LAG = 6
PV_LAG = 3
N_SCORE_SLOTS = SCORE_LAG + 2
N_PROB_SLOTS = PV_LAG + 2
MAX_SUB = 2
STEP_FIRST, STEP_LAST = 1, 2
SUB_SKIP, SUB_FAR, SUB_NEAR = "skip", "far", "near"


def _softmax_tiles(tiles, tk, tk_values, s_slots, p_slots, m_ref, alpha_ref):
    n = len(tiles)
    assert n <= alpha_ref.shape[0] and max(SCORE_LAG, PV_LAG) < N_TILES

    def by_sublanes(x):
        return x.reshape(x.shape[0] // SUBLANES, SUBLANES, x.shape[1])

    def scores(t):
        k, q_t, _, bias, _, _, stat = tiles[t]
        st = _dot(k(), q_t())
        if bias is not None:
            st = st + bias()
        s_slots[t % N_SCORE_SLOTS][0:tk, :] = st
        m_old = m_ref[stat]
        m_new = jnp.maximum(m_old, jnp.max(st, axis=0, keepdims=True))
        m_ref[stat] = m_new
        alpha_ref[t] = jnp.exp2(m_old - m_new)

    def probs(t):
        stat = tiles[t][6]
        s = by_sublanes(s_slots[t % N_SCORE_SLOTS][0:tk, :])
        p = jnp.exp2(s - m_ref[stat][None]).reshape(tk, s.shape[2])
        p_slots[t % N_PROB_SLOTS][0:tk, :] = p.astype(BF16)

    def values(t):
        _, _, v_t, _, acc_ref, idx, _ = tiles[t]
        pv = _dot(v_t(), p_slots[t % N_PROB_SLOTS][0:tk_values, :])
        acc = by_sublanes(acc_ref[idx]) * alpha_ref[t][None]
        acc_ref[idx] = acc.reshape(pv.shape) + pv

    for i in range(n + SCORE_LAG + PV_LAG):
        if i < n:
            scores(i)
        if 0 <= i - SCORE_LAG < n:
            probs(i - SCORE_LAG)
        if 0 <= i - SCORE_LAG - PV_LAG < n:
            values(i - SCORE_LAG - PV_LAG)


def _attn_kernel(qi_ref, ki_ref, var_ref, mvar_ref, flag_ref, kind_ref, *refs, has_meta, lam_init, sub, kinds):
    del qi_ref, ki_ref
    qa1_ref, qa2_ref, ka_ref, vat_ref, qb_ref, kb_ref, vbt_ref, bias_ref = refs[:8]
    refs = refs[8:]
    if has_meta:
        mka_ref, mvat_ref, mkb_ref, mvbt_ref, mbias_ref = refs[:5]
        refs = refs[5:]
    lam_ref, sg_ref, oa_ref, ob_ref, m_ref, alpha_ref, acca, accb = refs[:8]
    s_slots = refs[8:8 + N_SCORE_SLOTS]
    p_slots = refs[8 + N_SCORE_SLOTS:]

    step = pl.program_id(1)
    flags = flag_ref[step]
    first = (flags & STEP_FIRST) != 0
    last = (flags & STEP_LAST) != 0

    def tiles_of(k_ref, vt_ref, k2_ref, vt2_ref, b_ref, variant, keys):
        tiles = []
        for h in range(H_A):
            sl = slice(h * 2 * HD_A, (h + 1) * 2 * HD_A)
            vsl = slice(h * VA_PAD, (h + 1) * VA_PAD)
            bias = None if b_ref is None else (lambda h=h: b_ref[variant, h])
            for mp, q_ref in enumerate((qa1_ref, qa2_ref)):
                tiles.append((lambda sl=sl: k_ref[keys, sl], lambda sl=sl, q_ref=q_ref: q_ref[sl, :],
                              lambda vsl=vsl: vt_ref[vsl, keys], bias, acca, 2 * h + mp, 2 * h + mp))
        bias = None if b_ref is None else (lambda: b_ref[variant, H_A])
        for h in range(H_B):
            sl = slice(h * HB_PAD, (h + 1) * HB_PAD)
            vsl = slice(h * VB_PAD, (h + 1) * VB_PAD)
            tiles.append((lambda sl=sl: k2_ref[keys, sl], lambda sl=sl: qb_ref[sl, :],
                          lambda vsl=vsl: vt2_ref[vsl, keys], bias, accb, h, 2 * H_A + h))
        return tiles

    @pl.when(first)
    def _():
        m_ref[...] = jnp.full(m_ref.shape, MASKED, F32)
        acca[...] = jnp.zeros(acca.shape, F32)
        accb[...] = jnp.zeros(accb.shape, F32)
        if has_meta:
            for p_slot in p_slots:
                p_slot[0:LANES, :] = jnp.zeros((LANES, p_slot.shape[1]), BF16)
            tiles = tiles_of(mka_ref, mvat_ref, mkb_ref, mvbt_ref, mbias_ref, mvar_ref[step], slice(None))
            _softmax_tiles(tiles, N_META, LANES, s_slots, p_slots, m_ref, alpha_ref)

    for kind_id, classes in enumerate(kinds):
        @pl.when(kind_ref[step] == kind_id)
        def _(classes=classes):
            tiles = []
            for s, cls in enumerate(classes):
                if cls != SUB_SKIP:
                    tiles += tiles_of(ka_ref, vat_ref, kb_ref, vbt_ref, bias_ref if cls == SUB_NEAR else None,
                                      var_ref[step * MAX_SUB + s], slice(s * sub, (s + 1) * sub))
            _softmax_tiles(tiles, sub, sub, s_slots, p_slots, m_ref, alpha_ref)

    @pl.when(last)
    def _():
        al = lam_ref[...]
        lam = (jnp.exp(jnp.sum(al[0:1] * al[1:2], axis=1, keepdims=True))
               - jnp.exp(jnp.sum(al[2:3] * al[3:4], axis=1, keepdims=True)) + lam_init)
        sg = sg_ref[...]

        def normalised(acc_ref, idx, dv):
            a = acc_ref[idx]
            return a[0:dv] / a[dv:dv + 1]

        for h in range(H_A):
            o = normalised(acca, 2 * h, DV_A) - lam * normalised(acca, 2 * h + 1, DV_A)
            y = o * lax.rsqrt(jnp.mean(o * o, axis=0, keepdims=True) + EPS) * sg * (1.0 - lam_init)
            oa_ref[:, h * DV_A:(h + 1) * DV_A] = y.T.astype(BF16)
        for j in range(H_B // 2):
            pair = jnp.concatenate([normalised(accb, 2 * j, V_B), normalised(accb, 2 * j + 1, V_B)],
                                   axis=0)
            ob_ref[:, j * 2 * V_B:(j + 1) * 2 * V_B] = pair.T.astype(BF16)


def _t5_bucket(rel):
    half = NUM_BUCKETS // 2
    max_exact = half // 2
    n = jnp.abs(rel)
    nf = jnp.maximum(n, 1).astype(F32)
    large = max_exact + (jnp.log(nf / max_exact) / math.log(MAX_DISTANCE / max_exact)
                         * (half - max_exact)).astype(jnp.int32)
    large = jnp.minimum(large, half - 1)
    return jnp.where(rel > 0, half, 0) + jnp.where(n < max_exact, n, large)


def _bias_tile(table, q_pos, k_pos, k_valid):
    q_pos = jnp.asarray(q_pos, jnp.int32)
    k_pos = jnp.asarray(k_pos, jnp.int32)
    rel = k_pos[:, None] - q_pos[None, :]
    table = table.astype(F32)
    onehot = (_t5_bucket(rel)[..., None] == jnp.arange(NUM_BUCKETS)).astype(F32)
    shifted = (table - table[NUM_BUCKETS // 2 - 1]) * LOG2E
    bias = jnp.einsum("kqb,bh->hkq", onehot, shifted, precision=lax.Precision.HIGHEST)
    bias = jnp.concatenate([bias, jnp.zeros((1,) + rel.shape, F32)], axis=0)
    visible = ((k_pos[:, None] // CHUNK) <= (q_pos[None, :] // CHUNK)) & jnp.asarray(k_valid)[:, None]
    return jnp.where(visible[None], bias, MASKED)


def _plan_tiles(q_pos, k_pos, k_valid, tq, tk, sub, always_near):
    nq, nk, n_sub = len(q_pos) // tq, len(k_pos) // tk, tk // sub
    pairs, variants, defs = [], {}, []
    for qi in range(nq):
        qp = q_pos[qi * tq:(qi + 1) * tq]
        for ki in range(nk):
            classes, var = [], []
            for s in range(n_sub):
                rows = slice(ki * tk + s * sub, ki * tk + (s + 1) * sub)
                kp, kv = k_pos[rows], k_valid[rows]
                vis = ((kp[:, None] // CHUNK) <= (qp[None, :] // CHUNK)) & kv[:, None]
                rel = kp[:, None] - qp[None, :]
                if not (always_near or vis.any()):
                    classes.append(SUB_SKIP)
                    var.append(0)
                elif vis.all() and rel.max() <= -FAR_REL and not always_near:
                    classes.append(SUB_FAR)
                    var.append(0)
                else:
                    key = (int(kp[0] - qp[0]), int(qp[0] % CHUNK), int(kp[0] % CHUNK), int(kv.sum()),
                           tuple(np.diff(qp).tolist()) if np.any(np.diff(qp) != 1) else ())
                    if key not in variants:
                        variants[key] = len(defs)
                        defs.append((qp, kp, kv))
                    classes.append(SUB_NEAR)
                    var.append(variants[key])
            if any(c != SUB_SKIP for c in classes):
                pairs.append((qi, ki, tuple(classes), var))
    return pairs, defs


def _attention_plan(table, q_pos, k_pos, k_valid, tq, tk, sub, has_meta):
    n_sub = tk // sub
    assert tk % sub == 0 and n_sub <= MAX_SUB
    pairs, defs = _plan_tiles(q_pos, k_pos, k_valid, tq, tk, sub, always_near=False)
    if not defs:
        defs = [(q_pos[:tq], k_pos[:sub], k_valid[:sub])]
    bias = jnp.stack([_bias_tile(table, *d) for d in defs])
    n = len(pairs)
    qi = np.array([p[0] for p in pairs], np.int32)
    ki = np.array([p[1] for p in pairs], np.int32)
    kinds = sorted(set(p[2] for p in pairs))
    kind = np.array([kinds.index(p[2]) for p in pairs], np.int32)
    var = np.zeros((n, MAX_SUB), np.int32)
    for s, p in enumerate(pairs):
        var[s, :n_sub] = p[3]
    flags = np.zeros(n, np.int32)
    for s in range(n):
        if s == 0 or qi[s] != qi[s - 1]:
            flags[s] |= STEP_FIRST
        if s == n - 1 or qi[s] != qi[s + 1]:
            flags[s] |= STEP_LAST
    assert sorted(set(qi.tolist())) == list(range(len(q_pos) // tq))
    plan = dict(tq=tq, tk=tk, sub=sub, kinds=tuple(kinds), qi=qi, ki=ki, var=var.reshape(-1), flags=flags,
                kind=kind, bias=bias, mvar=np.zeros(n, np.int32), mbias=None)
    if has_meta:
        mpairs, mdefs = _plan_tiles(q_pos, np.arange(-N_META, 0), np.ones(N_META, bool), tq, N_META, N_META,
                                    always_near=True)
        plan["mbias"] = jnp.stack([_bias_tile(table, *d) for d in mdefs])
        plan["mvar"] = np.array([p[3][0] for p in mpairs], np.int32)[qi]
    return plan


def _attention(p, meta, plan, a_lambda, subln_g, lam_init):
    b, da, lq = p["qa1"].shape
    hv = H_B * V_B
    tq, tk, sub = plan["tq"], plan["tk"], plan["sub"]
    n_steps = len(plan["qi"])
    has_meta = meta is not None

    qblk = lambda w: pl.BlockSpec((None, tq, w), lambda i, s, qi_, *_: (i, qi_[s], 0))
    qtblk = lambda w: pl.BlockSpec((None, w, tq), lambda i, s, qi_, *_: (i, 0, qi_[s]))
    kblk = lambda w: pl.BlockSpec((None, tk, w), lambda i, s, qi_, ki_, *_: (i, ki_[s], 0))
    ktblk = lambda w: pl.BlockSpec((None, w, tk), lambda i, s, qi_, ki_, *_: (i, 0, ki_[s]))
    slot_rows = max(sub, LANES)
    in_specs = [qtblk(da), qtblk(da), kblk(da), ktblk(H_A * VA_PAD), qtblk(H_B * HB_PAD), kblk(H_B * HB_PAD),
                ktblk(H_B * VB_PAD), _const_spec(plan["bias"].shape)]
    args = [p["qa1"], p["qa2"], p["ka"], p["vat"], p["qb"], p["kb"], p["vbt"], plan["bias"]]
    if has_meta:
        margs = [meta["ka"], meta["vat"], meta["kb"], meta["vbt"], plan["mbias"]]
        in_specs += [_const_spec(a.shape) for a in margs]
        args += margs
    in_specs += [_const_spec(a_lambda.shape), _const_spec((DV_A, 1))]
    args += [a_lambda, subln_g.reshape(DV_A, 1)]
    grid_spec = pltpu.PrefetchScalarGridSpec(
        num_scalar_prefetch=6, grid=(b, n_steps), in_specs=in_specs,
        out_specs=[qblk(da), qblk(hv)],
        scratch_shapes=[pltpu.VMEM((N_TILES, SUBLANES, tq), F32),
                        pltpu.VMEM((MAX_SUB * N_TILES, SUBLANES, tq), F32),
                        pltpu.VMEM((2 * H_A, VA_PAD, tq), F32), pltpu.VMEM((H_B, VB_PAD, tq), F32)]
        + [pltpu.VMEM((slot_rows, tq), F32)] * N_SCORE_SLOTS
        + [pltpu.VMEM((slot_rows, tq), BF16)] * N_PROB_SLOTS)
    oa, ob = pl.pallas_call(
        functools.partial(_attn_kernel, has_meta=has_meta, lam_init=lam_init, sub=sub, kinds=plan["kinds"]),
        grid_spec=grid_spec,
        out_shape=[jax.ShapeDtypeStruct((b, lq, da), BF16), jax.ShapeDtypeStruct((b, lq, hv), BF16)],
        compiler_params=pltpu.CompilerParams(dimension_semantics=("parallel", "arbitrary"),
                                             vmem_limit_bytes=VMEM_LIMIT),
    )(*(jnp.asarray(plan[k]) for k in ("qi", "ki", "var", "mvar", "flags", "kind")), *args)
    return oa, ob


def _post_kernel(x_ref, oa_ref, ob_ref, gmix_ref, wg_ref, wabr_ref, wbbr_ref, wout_ref, gffn_ref, wup_ref,
                 cw_ref, cb_ref, wdn_ref, cbuf_ref, h_ref, cst_ref, acc_ref, ext0_ref, ext1_ref, carry_ref,
                 *, tm, fc, n_chunks):
    j = pl.program_id(1)
    d = x_ref.shape[-1]
    x = x_ref[...]
    hb = _rms(x, gmix_ref[...]).astype(BF16)
    gates = jax.nn.sigmoid(_dot(hb, wg_ref[...]))
    m = gates[:, :d] * _dot(oa_ref[...], wabr_ref[...]) + gates[:, d:] * _dot(ob_ref[...], wbbr_ref[...])
    h = x + _dot(m.astype(BF16), wout_ref[...])
    hb2 = _rms(h, gffn_ref[...]).astype(BF16)

    @pl.when(j == 0)
    def _():
        carry_ref[...] = cbuf_ref[...]

    acc_ref[...] = h
    ext_refs = (ext0_ref, ext1_ref)
    chunk_cols = lambda c: slice(c * 2 * fc, (c + 1) * 2 * fc)

    def up_project(c):
        ext_refs[c % 2][SUBLANES:SUBLANES + tm, :] = _dot(hb2, wup_ref[:, chunk_cols(c)])

    up_project(0)
    for c in range(n_chunks):
        cols = chunk_cols(c)
        ext_ref = ext_refs[c % 2]
        if c + 1 < n_chunks:
            up_project(c + 1)
        ext_ref[0:SUBLANES, :] = carry_ref[:, cols]
        u = ext_ref[SUBLANES:SUBLANES + tm, :]
        cw = cw_ref[:, cols]
        conv = (cb_ref[:, cols] + cw[0:1] * ext_ref[SUBLANES - 2:SUBLANES - 2 + tm, :]
                + cw[1:2] * ext_ref[SUBLANES - 1:SUBLANES - 1 + tm, :] + cw[2:3] * u)
        carry_ref[:, cols] = ext_ref[tm:tm + SUBLANES, :]
        a_up = conv[:, :fc]
        g_up = conv[:, fc:]
        act = (g_up * jax.nn.sigmoid(g_up) * a_up).astype(BF16)
        acc_ref[...] += _dot(act, wdn_ref[c * fc:(c + 1) * fc, :])
    h_ref[...] = acc_ref[...]

    @pl.when(j == pl.num_programs(1) - 1)
    def _():
        cst_ref[...] = carry_ref[...]


def _post(x, oa, ob, cbuf8, lw, tm):
    b, L, d = x.shape
    ff2 = lw["wup"].shape[1]
    fc = lw["fc"]
    n_chunks = ff2 // (2 * fc)
    row = lambda w: pl.BlockSpec((None, tm, w), lambda i, j: (i, j, 0))
    st = pl.BlockSpec((None, SUBLANES, ff2), lambda i, j: (i, 0, 0))
    weights = [lw["g_mix"], lw["wg"], lw["wabr"], lw["wbbr"], lw["wout"], lw["g_ffn"], lw["wup"],
               lw["conv_w"], lw["conv_b"], lw["wdn"]]
    return pl.pallas_call(
        functools.partial(_post_kernel, tm=tm, fc=fc, n_chunks=n_chunks),
        grid=(b, L // tm),
        in_specs=[row(d), row(oa.shape[-1]), row(ob.shape[-1])] + [_const_spec(w.shape) for w in weights] + [st],
        out_specs=[row(d), st],
        out_shape=[jax.ShapeDtypeStruct((b, L, d), F32), jax.ShapeDtypeStruct((b, SUBLANES, ff2), F32)],
        scratch_shapes=[pltpu.VMEM((tm, d), F32), pltpu.VMEM((tm + SUBLANES, 2 * fc), F32),
                        pltpu.VMEM((tm + SUBLANES, 2 * fc), F32), pltpu.VMEM((SUBLANES, ff2), F32)],
        compiler_params=pltpu.CompilerParams(dimension_semantics=("parallel", "arbitrary"),
                                             vmem_limit_bytes=VMEM_LIMIT),
    )(x, oa, ob, *weights, cbuf8)


def _norm_kernel(x_ref, g_ref, o_ref):
    o_ref[...] = _rms(x_ref[...], g_ref[...])


def _final_norm(x, g, tm):
    b, L, d = x.shape
    row = pl.BlockSpec((None, tm, d), lambda i, j: (i, j, 0))
    return pl.pallas_call(
        _norm_kernel, grid=(b, L // tm), in_specs=[row, _const_spec((1, d))], out_specs=row,
        out_shape=jax.ShapeDtypeStruct((b, L, d), F32),
        compiler_params=pltpu.CompilerParams(dimension_semantics=("parallel", "parallel")),
    )(x, g.reshape(1, d))


def _layer_weights(l, norm_mix_g, w_in, b_q_norm_g, w_b_uq, b_kv_norm_g, w_b_ukv, w_a_br, w_b_br, w_out,
                   norm_ffn_g, w_up, conv_w, conv_b, w_down):
    d = w_in.shape[1]
    da = H_A * 2 * HD_A
    q_lora = w_b_uq.shape[1]
    kvl = w_b_ukv.shape[1]
    d_ff = w_down.shape[1]
    w = w_in[l]
    o_cq = 3 * da
    o_ckv = o_cq + q_lora
    o_kr = o_ckv + kvl
    o_g = o_kr + ROPE_B
    bf = lambda a: a.astype(BF16)
    wv = w[:, 2 * da:3 * da]
    wkr = jnp.zeros((d, LANES), F32).at[:, ROPE_LO:ROPE_LO + ROPE_B].set(w[:, o_kr:o_g])
    wuq = jnp.pad(w_b_uq[l], ((0, 0), (0, 0), (0, HB_PAD - NOPE_B - ROPE_B))).reshape(q_lora, H_B * HB_PAD)
    wuk = jnp.pad(w_b_ukv[l][:, :, :NOPE_B], ((0, 0), (0, 0), (0, HB_PAD - NOPE_B))).reshape(kvl, H_B * HB_PAD)
    wuv = w_b_ukv[l][:, :, NOPE_B:].reshape(kvl, H_B * V_B)
    fc = 256 if d_ff % 256 == 0 else d_ff
    nch = d_ff // fc
    regroup = lambda a: a.reshape(a.shape[0], 2, nch, fc).transpose(0, 2, 1, 3).reshape(a.shape[0], 2 * d_ff)
    return dict(
        g_mix=norm_mix_g[l].reshape(1, d), wq=bf(w[:, :da]), wq_t=bf(w[:, :da].T), wk=bf(w[:, da:2 * da]),
        wv=bf(wv), wv_t=bf(wv.T),
        wcq=bf(w[:, o_cq:o_ckv]), wckv=bf(w[:, o_ckv:o_kr]), wkr=bf(wkr), wg=bf(w[:, o_g:]),
        g_q=b_q_norm_g[l].reshape(1, q_lora), wuq=bf(wuq), wuq_t=bf(wuq.T),
        g_kv=b_kv_norm_g[l].reshape(1, kvl), wuk=bf(wuk),
        wuv=bf(wuv), wuv_t=bf(wuv.T), wabr=bf(w_a_br[l]), wbbr=bf(w_b_br[l]), wout=bf(w_out[l]),
        g_ffn=norm_ffn_g[l].reshape(1, d), wup=bf(regroup(w_up[l])),
        conv_w=jnp.pad(regroup(conv_w[l]), ((0, SUBLANES - CONV_W), (0, 0))),
        conv_b=regroup(conv_b[l].reshape(1, 2 * d_ff)), wdn=bf(w_down[l]), fc=fc, nch=nch, d_ff=d_ff)


def _ungroup_conv_state(cst, lw):
    b = cst.shape[0]
    s = cst[:, SUBLANES - (CONV_W - 1):, :].reshape(b, CONV_W - 1, lw["nch"], 2, lw["fc"])
    return s.transpose(0, 1, 3, 2, 4).reshape(b, CONV_W - 1, 2 * lw["d_ff"])


def _group_conv_state(buf, lw):
    b = buf.shape[0]
    s = buf.reshape(b, CONV_W - 1, 2, lw["nch"], lw["fc"]).transpose(0, 1, 3, 2, 4)
    s = s.reshape(b, CONV_W - 1, 2 * lw["d_ff"])
    return jnp.pad(s, ((0, 0), (SUBLANES - (CONV_W - 1), 0), (0, 0)))


def _rope_tables(pos):
    inv = ROPE_THETA ** (-jnp.arange(ROPE_HALF, dtype=F32) / ROPE_HALF)
    ang = jnp.asarray(pos).astype(F32)[:, None] * inv[None, :]
    c, s = jnp.cos(ang), jnp.sin(ang)
    z = jnp.zeros_like(c)
    n = pos.shape[0]
    tail = jnp.zeros((n, LANES - ROPE_LO - ROPE_B), F32)
    head0 = jnp.zeros((n, ROPE_LO), F32)
    cos = jnp.concatenate([jnp.ones((n, ROPE_LO), F32), c, c, tail], axis=1)
    sin_lo = jnp.concatenate([head0, -s, z, tail], axis=1)
    sin_hi = jnp.concatenate([head0, z, s, tail], axis=1)
    return cos, sin_lo, sin_hi, c.T, s.T


def _pad_rows(a, n, axis):
    pad = [(0, 0)] * a.ndim
    pad[axis] = (0, n - a.shape[axis])
    return jnp.pad(a, pad)


def _feature_major_values(v, heads, dv):
    b, rows, _ = v.shape
    vt = jnp.swapaxes(v, 1, 2).reshape(b, heads, dv, rows)
    extra = jnp.zeros((b, heads, BF16_ROWS, rows), BF16).at[:, :, 0, :].set(1.0)
    return jnp.concatenate([vt, extra], axis=2).reshape(b, heads * (dv + BF16_ROWS), rows)


def _row_tile(L, want):
    t = min(L, want)
    while L % t:
        t //= 2
    return t


def kernel(x_prompt, x_sample, cache_a_k, cache_a_v, cache_b_ckv, cache_b_krope, state_ffn_conv, meta_tokens,
           rel_bias_table, norm_mix_g, w_in, a_lambda, a_subln_g, b_q_norm_g, w_b_uq, b_kv_norm_g, w_b_ukv,
           w_a_br, w_b_br, w_out, norm_ffn_g, w_up, conv_w, conv_b, w_down, final_norm_g):
    bp, seq, d = x_prompt.shape
    bs, dec, _ = x_sample.shape
    depth = w_in.shape[0]
    past = cache_a_k.shape[2]
    da = H_A * 2 * HD_A
    ff2 = w_up.shape[2]
    assert seq % CHUNK == 0 and N_META % SUBLANES == 0 and dec % SUBLANES == 0

    meta_pos = np.arange(-N_META, 0)
    prompt_pos = np.arange(seq)
    sample_pos = np.arange(past, past + dec)
    tabs_m, tabs_p, tabs_s = _rope_tables(meta_pos), _rope_tables(prompt_pos), _rope_tables(sample_pos)

    tm_p = _row_tile(seq, 256)
    tq_p = sub_p = _row_tile(seq, 256)
    tk_p = MAX_SUB * sub_p if seq % (MAX_SUB * sub_p) == 0 else sub_p
    qpad = lambda a: _pad_rows(jnp.swapaxes(a, 1, 2), LANES, 2)
    mq_pos = np.concatenate([meta_pos, np.full(LANES - N_META, -1)])
    mk_valid = np.arange(LANES) < N_META
    sq_pos = np.concatenate([sample_pos, np.full(LANES - dec, sample_pos[-1])])
    lk_s = -(-(past + dec) // LANES) * LANES
    sk_pos = np.concatenate([np.arange(past + dec), np.zeros(lk_s - past - dec, np.int64)])
    sk_valid = np.arange(lk_s) < past + dec
    tm_c = _row_tile(past, 256)
    plan_m = _attention_plan(rel_bias_table, mq_pos, mq_pos, mk_valid, LANES, LANES, LANES, has_meta=False)
    plan_p = _attention_plan(rel_bias_table, prompt_pos, prompt_pos, np.ones(seq, bool), tq_p, tk_p, sub_p,
                             has_meta=True)
    plan_s = _attention_plan(rel_bias_table, sq_pos, sk_pos, sk_valid, LANES, lk_s, lk_s, has_meta=True)

    h_m = meta_tokens[None].astype(F32)
    h_p, h_s = x_prompt, x_sample
    outs = {k: [] for k in ("ak_s", "av_s", "ck_s", "kr_s", "cv_p", "cv_s")}
    meta_rows = {name: [] for name in CACHE_NAMES}
    cache_p = None
    for l in range(depth):
        lam_init = 0.8 - 0.6 * math.exp(-0.3 * l)
        lw = _layer_weights(l, norm_mix_g, w_in, b_q_norm_g, w_b_uq, b_kv_norm_g, w_b_ukv, w_a_br, w_b_br,
                            w_out, norm_ffn_g, w_up, conv_w, conv_b, w_down)
        attn = functools.partial(_attention, a_lambda=a_lambda[l], subln_g=a_subln_g[l], lam_init=lam_init)

        pm = _projection(h_m, lw, tabs_m, N_META, transposed=False)
        meta = dict(ka=pm["k16"][0], vat=_pad_rows(_feature_major_values(pm["v16"], H_A, DV_A)[0], LANES, 1),
                    kb=pm["kb"][0], vbt=_pad_rows(_feature_major_values(pm["vb"], H_B, V_B)[0], LANES, 1))
        qm = dict(qa1=qpad(pm["qa1"]), qa2=qpad(pm["qa2"]), qb=qpad(pm["qb"]),
                  ka=_pad_rows(meta["ka"], LANES, 0)[None], vat=meta["vat"][None],
                  kb=_pad_rows(meta["kb"], LANES, 0)[None], vbt=meta["vbt"][None])
        oa, ob = attn(qm, None, plan_m)
        h_m, cst_m = _post(h_m, oa[:, :N_META], ob[:, :N_META], jnp.zeros((1, SUBLANES, ff2), F32), lw, N_META)

        pp = _projection(h_p, lw, tabs_p, tm_p, transposed=True, cache=(depth, l, N_META, cache_p))
        cache_p = {name: pp[name] for name in CACHE_NAMES}
        qp = dict(qa1=pp["qa1"], qa2=pp["qa2"], qb=pp["qb"], ka=pp["k16"], vat=pp["v16"], kb=pp["kb"],
                  vbt=pp["vb"])
        oa, ob = attn(qp, meta, plan_p)
        h_p, cst_p = _post(h_p, oa, ob, jnp.broadcast_to(cst_m, (bp, SUBLANES, ff2)), lw, tm_p)

        ps = _projection(h_s, lw, tabs_s, dec, transposed=False)
        kr_cache = jnp.pad(cache_b_krope[l], ((0, 0), (0, 0), (ROPE_LO, LANES - ROPE_LO - ROPE_B)))
        kb_c, vbt_c = _kv_up_cache(cache_b_ckv[l], kr_cache, lw, tm_c)
        va_all = jnp.concatenate([cache_a_v[l].reshape(bs, past, da).astype(BF16), ps["v16"]], axis=1)
        vb_all = jnp.concatenate([jnp.swapaxes(vbt_c, 1, 2), ps["vb"]], axis=1)
        qs = dict(qa1=qpad(ps["qa1"]), qa2=qpad(ps["qa2"]), qb=qpad(ps["qb"]),
                  ka=_pad_rows(jnp.concatenate([cache_a_k[l].reshape(bs, past, da).astype(BF16), ps["k16"]],
                                               axis=1), lk_s, 1),
                  vat=_pad_rows(_feature_major_values(va_all, H_A, DV_A), lk_s, 2),
                  kb=_pad_rows(jnp.concatenate([kb_c, ps["kb"]], axis=1), lk_s, 1),
                  vbt=_pad_rows(_feature_major_values(vb_all, H_B, V_B), lk_s, 2))
        oa, ob = attn(qs, meta, plan_s)
        h_s, cst_s = _post(h_s, oa[:, :dec], ob[:, :dec], _group_conv_state(state_ffn_conv[l], lw), lw, dec)

        heads = lambda a: a.reshape(a.shape[:-1] + (H_A, 2 * HD_A))
        for name in CACHE_NAMES:
            meta_rows[name].append(pm[name])
        outs["cv_p"].append(_ungroup_conv_state(cst_p, lw))
        outs["ak_s"].append(heads(ps["kf"]))
        outs["av_s"].append(heads(ps["vf"]))
        outs["ck_s"].append(ps["ckv"])
        outs["kr_s"].append(ps["kr"])
        outs["cv_s"].append(_ungroup_conv_state(cst_s, lw))

    y_p = _final_norm(h_p, final_norm_g, tm_p)
    y_s = _final_norm(h_s, final_norm_g, dec)
    st = lambda k: jnp.stack(outs[k])

    def with_meta_rows(name):
        rows = jnp.stack(meta_rows[name])
        rows = jnp.broadcast_to(rows, (depth, bp) + rows.shape[2:])
        return cache_p[name].at[:, :, :N_META].set(rows)

    return (y_p, y_s, heads(with_meta_rows("kf")), st("ak_s"), heads(with_meta_rows("vf")), st("av_s"),
            with_meta_rows("ckv"), st("ck_s"), with_meta_rows("kr"), st("kr_s"), st("cv_p"), st("cv_s"))
```

```python
import functools
import math

import numpy as np
import jax
import jax.numpy as jnp
from jax import lax
from jax.experimental import pallas as pl
from jax.experimental.pallas import tpu as pltpu

CHUNK = 64
N_META = 16
EPS = 1e-6
H_A = 8
HD_A = 64
H_B = 8
NOPE_B = 64
ROPE_B = 32
V_B = 64
ROPE_THETA = 10000.0
NUM_BUCKETS = 32
MAX_DISTANCE = 128
CONV_W = 3

LANES = 128
SUBLANES = 8
BF16_ROWS = 16
LOG2E = 1.4426950408889634
MASKED = -1e30
FAR_REL = 91
VMEM_LIMIT = 56 * 1024 * 1024

F32 = jnp.float32
BF16 = jnp.bfloat16
HB_PAD = LANES
ROPE_LO = NOPE_B
ROPE_HALF = ROPE_B // 2
DV_A = 2 * HD_A
VA_PAD = DV_A + BF16_ROWS
VB_PAD = V_B + BF16_ROWS


def _dot(a, b):
    return jnp.dot(a, b, preferred_element_type=F32)


def _dot_nt(a, b):
    return lax.dot_general(a, b, (((1,), (1,)), ((), ())), preferred_element_type=F32)


def _rms(x, g):
    return x * lax.rsqrt(jnp.mean(x * x, axis=-1, keepdims=True) + EPS) * g


def _rope_padded(x, cos, sin_lo, sin_hi):
    return (x * cos + pltpu.roll(x, LANES - ROPE_HALF, 1) * sin_lo
            + pltpu.roll(x, ROPE_HALF, 1) * sin_hi)


def _const_spec(shape):
    nd = len(shape)
    return pl.BlockSpec(shape, lambda *_: (0,) * nd, pipeline_mode=pl.Buffered(1))


def _sum_rows(n):
    r = lax.broadcasted_iota(jnp.int32, (BF16_ROWS, n), 0)
    return jnp.where(r == 0, 1.0, 0.0).astype(BF16)


N_PROJ_INPUTS = 18


def _proj_kernel(*refs, transposed, n_alias):
    (x_ref, g_ref, wq_ref, wk_ref, wv_ref, wcq_ref, wckv_ref, wkr_ref, qng_ref,
     wuq_ref, kvng_ref, wuk_ref, wuv_ref, cos_ref, slo_ref, shi_ref, cost_ref, sint_ref) = refs[:N_PROJ_INPUTS]
    (qa1_ref, qa2_ref, kf_ref, k16_ref, vf_ref, v16_ref, qb_ref, ckv_ref, kr_ref, kb_ref,
     vb_ref) = refs[N_PROJ_INPUTS + n_alias:]
    x = x_ref[...]
    rows = x.shape[0]
    hb = _rms(x, g_ref[...]).astype(BF16)
    scale_a = HD_A ** -0.5 * LOG2E
    scale_b = (NOPE_B + ROPE_B) ** -0.5 * LOG2E
    cos = cos_ref[...]
    slo = slo_ref[...]
    shi = shi_ref[...]

    def store_rows(ref, val):
        ref[(0,) * (len(ref.shape) - 2)] = val

    v = _dot(hb, wv_ref[...])
    store_rows(vf_ref, v)
    if transposed:
        vt = v.T.astype(BF16)
        ones = _sum_rows(rows)
        for h in range(H_A):
            v16_ref[h * VA_PAD:h * VA_PAD + DV_A, :] = vt[h * DV_A:(h + 1) * DV_A]
            v16_ref[h * VA_PAD + DV_A:(h + 1) * VA_PAD, :] = ones
    else:
        v16_ref[...] = v.astype(BF16)
    k = _dot(hb, wk_ref[...])
    store_rows(kf_ref, k)
    k16_ref[...] = k.astype(BF16)

    if transposed:
        qa = _dot_nt(wq_ref[...], hb) * scale_a
        feat = lax.broadcasted_iota(jnp.int32, qa.shape, 0) % (2 * HD_A)
    else:
        qa = _dot(hb, wq_ref[...]) * scale_a
        feat = lax.broadcasted_iota(jnp.int32, qa.shape, 1) % (2 * HD_A)
    qa1_ref[...] = jnp.where(feat < HD_A, qa, 0.0).astype(BF16)
    qa2_ref[...] = jnp.where(feat >= HD_A, qa, 0.0).astype(BF16)

    cq = _rms(_dot(hb, wcq_ref[...]), qng_ref[...]).astype(BF16)
    if transposed:
        qb = _dot_nt(wuq_ref[...], cq) * scale_b
        ct = cost_ref[...]
        st = sint_ref[...]
        for h in range(H_B):
            lo = h * HB_PAD + ROPE_LO
            x1 = qb[lo:lo + ROPE_HALF]
            x2 = qb[lo + ROPE_HALF:lo + ROPE_B]
            qb_ref[h * HB_PAD:lo, :] = qb[h * HB_PAD:lo].astype(BF16)
            qb_ref[lo:lo + ROPE_HALF, :] = (x1 * ct - x2 * st).astype(BF16)
            qb_ref[lo + ROPE_HALF:lo + ROPE_B, :] = (x1 * st + x2 * ct).astype(BF16)
            qb_ref[lo + ROPE_B:(h + 1) * HB_PAD, :] = qb[lo + ROPE_B:(h + 1) * HB_PAD].astype(BF16)
    else:
        qb = _dot(cq, wuq_ref[...]) * scale_b
        for h in range(H_B):
            sl = slice(h * HB_PAD, (h + 1) * HB_PAD)
            qb_ref[:, sl] = _rope_padded(qb[:, sl], cos, slo, shi).astype(BF16)

    ckv = _rms(_dot(hb, wckv_ref[...]), kvng_ref[...])
    store_rows(ckv_ref, ckv)
    kr = _rope_padded(_dot(hb, wkr_ref[...]), cos, slo, shi)
    store_rows(kr_ref, kr[:, ROPE_LO:ROPE_LO + ROPE_B])
    cb = ckv.astype(BF16)
    kn = _dot(cb, wuk_ref[...])
    for h in range(H_B):
        sl = slice(h * HB_PAD, (h + 1) * HB_PAD)
        kb_ref[:, sl] = (kn[:, sl] + kr).astype(BF16)
    if transposed:
        vbt = _dot_nt(wuv_ref[...], cb).astype(BF16)
        ones = _sum_rows(rows)
        for h in range(H_B):
            vb_ref[h * VB_PAD:h * VB_PAD + V_B, :] = vbt[h * V_B:(h + 1) * V_B]
            vb_ref[h * VB_PAD + V_B:(h + 1) * VB_PAD, :] = ones
    else:
        vb_ref[...] = _dot(cb, wuv_ref[...]).astype(BF16)


CACHE_NAMES = ("kf", "vf", "ckv", "kr")


def _projection(x, lw, tabs, tm, transposed, cache=None):
    b, L, d = x.shape
    da = lw["wq"].shape[1]
    kvl = lw["wckv"].shape[1]
    hv = H_B * V_B
    hq = H_B * HB_PAD
    nj = L // tm
    row = lambda w: pl.BlockSpec((None, tm, w), lambda i, j: (i, j, 0))
    col = lambda w: pl.BlockSpec((None, w, tm), lambda i, j: (i, 0, j))
    tab = pl.BlockSpec((tm, LANES), lambda i, j: (j, 0))
    tab_t = pl.BlockSpec((ROPE_HALF, tm), lambda i, j: (0, j))
    t = "_t" if transposed else ""
    weights = [lw["g_mix"], lw["wq" + t], lw["wk"], lw["wv"], lw["wcq"], lw["wckv"], lw["wkr"],
               lw["g_q"], lw["wuq" + t], lw["g_kv"], lw["wuk"], lw["wuv" + t]]
    in_specs = [row(d)] + [_const_spec(w.shape) for w in weights] + [tab, tab, tab, tab_t, tab_t]
    sds = jax.ShapeDtypeStruct
    if transposed:
        feat = lambda w, wt: (sds((b, wt, L), BF16), col(wt))
    else:
        feat = lambda w, wt: (sds((b, L, w), BF16), row(w))
    if cache is None:
        f32row = lambda w: (sds((b, L, w), F32), row(w))
    else:
        depth, layer, lead, bufs = cache
        f32row = lambda w: (sds((depth, b, lead + L, w), F32),
                            pl.BlockSpec((pl.Element(1), pl.Element(1), pl.Element(tm), pl.Element(w)),
                                         lambda i, j: (layer, i, pl.multiple_of(lead + j * tm, SUBLANES), 0)))
    outs = [feat(da, da), feat(da, da), f32row(da), (sds((b, L, da), BF16), row(da)),
            f32row(da), feat(da, H_A * VA_PAD), feat(hq, hq), f32row(kvl), f32row(ROPE_B),
            (sds((b, L, hq), BF16), row(hq)), feat(hv, H_B * VB_PAD)]
    names = ("qa1", "qa2", "kf", "k16", "vf", "v16", "qb", "ckv", "kr", "kb", "vb")
    args = [x, *weights, *tabs]
    aliases = {}
    if cache is not None and bufs is not None:
        for name in CACHE_NAMES:
            aliases[len(args)] = names.index(name)
            args.append(bufs[name])
            in_specs.append(pl.BlockSpec(memory_space=pl.ANY))
    res = pl.pallas_call(
        functools.partial(_proj_kernel, transposed=transposed, n_alias=len(aliases)),
        grid=(b, nj), in_specs=in_specs, out_specs=[s for _, s in outs], out_shape=[s for s, _ in outs],
        input_output_aliases=aliases,
        compiler_params=pltpu.CompilerParams(dimension_semantics=("parallel", "parallel"),
                                             vmem_limit_bytes=VMEM_LIMIT),
    )(*args)
    return dict(zip(names, res))


def _kvup_kernel(ckv_ref, kr_ref, wuk_ref, wuvt_ref, kb_ref, vbt_ref):
    cb = ckv_ref[...].astype(BF16)
    kn = _dot(cb, wuk_ref[...])
    kr = kr_ref[...]
    for h in range(H_B):
        sl = slice(h * HB_PAD, (h + 1) * HB_PAD)
        kb_ref[:, sl] = (kn[:, sl] + kr).astype(BF16)
    vbt_ref[...] = _dot_nt(wuvt_ref[...], cb).astype(BF16)


def _kv_up_cache(ckv, kr_pad, lw, tm):
    b, p, kvl = ckv.shape
    hv = H_B * V_B
    row = lambda w: pl.BlockSpec((None, tm, w), lambda i, j: (i, j, 0))
    return pl.pallas_call(
        _kvup_kernel, grid=(b, p // tm),
        in_specs=[row(kvl), row(LANES), _const_spec(lw["wuk"].shape), _const_spec(lw["wuv_t"].shape)],
        out_specs=[row(H_B * HB_PAD), pl.BlockSpec((None, hv, tm), lambda i, j: (i, 0, j))],
        out_shape=[jax.ShapeDtypeStruct((b, p, H_B * HB_PAD), BF16), jax.ShapeDtypeStruct((b, hv, p), BF16)],
        compiler_params=pltpu.CompilerParams(dimension_semantics=("parallel", "parallel"),
                                             vmem_limit_bytes=VMEM_LIMIT),
    )(ckv, kr_pad, lw["wuk"], lw["wuv_t"])


N_TILES = 2 * H_A + H_B
SCORE_LAG = 6
PV_LAG = 3
N_SCORE_SLOTS = SCORE_LAG + 2
N_PROB_SLOTS = PV_LAG + 2
MAX_SUB = 2
STEP_FIRST, STEP_LAST = 1, 2
SUB_SKIP, SUB_FAR, SUB_NEAR = "skip", "far", "near"


def _softmax_tiles(tiles, tk, tk_values, s_slots, p_slots, m_ref, alpha_ref):
    n = len(tiles)
    assert n <= alpha_ref.shape[0] and max(SCORE_LAG, PV_LAG) < N_TILES

    def by_sublanes(x):
        return x.reshape(x.shape[0] // SUBLANES, SUBLANES, x.shape[1])

    def scores(t):
        k, q_t, _, bias, _, _, stat = tiles[t]
        st = _dot(k(), q_t())
        if bias is not None:
            st = st + bias()
        s_slots[t % N_SCORE_SLOTS][0:tk, :] = st
        m_old = m_ref[stat]
        m_new = jnp.maximum(m_old, jnp.max(st, axis=0, keepdims=True))
        m_ref[stat] = m_new
        alpha_ref[t] = jnp.exp2(m_old - m_new)

    def probs(t):
        stat = tiles[t][6]
        s = by_sublanes(s_slots[t % N_SCORE_SLOTS][0:tk, :])
        p = jnp.exp2(s - m_ref[stat][None]).reshape(tk, s.shape[2])
        p_slots[t % N_PROB_SLOTS][0:tk, :] = p.astype(BF16)

    def values(t):
        _, _, v_t, _, acc_ref, idx, _ = tiles[t]
        pv = _dot(v_t(), p_slots[t % N_PROB_SLOTS][0:tk_values, :])
        acc = by_sublanes(acc_ref[idx]) * alpha_ref[t][None]
        acc_ref[idx] = acc.reshape(pv.shape) + pv

    for i in range(n + SCORE_LAG + PV_LAG):
        if i < n:
            scores(i)
        if 0 <= i - SCORE_LAG < n:
            probs(i - SCORE_LAG)
        if 0 <= i - SCORE_LAG - PV_LAG < n:
            values(i - SCORE_LAG - PV_LAG)


def _attn_kernel(qi_ref, ki_ref, var_ref, mvar_ref, flag_ref, kind_ref, *refs, has_meta, lam_init, sub, kinds):
    del qi_ref, ki_ref
    qa1_ref, qa2_ref, ka_ref, vat_ref, qb_ref, kb_ref, vbt_ref, bias_ref = refs[:8]
    refs = refs[8:]
    if has_meta:
        mka_ref, mvat_ref, mkb_ref, mvbt_ref, mbias_ref = refs[:5]
        refs = refs[5:]
    lam_ref, sg_ref, oa_ref, ob_ref, m_ref, alpha_ref, acca, accb = refs[:8]
    s_slots = refs[8:8 + N_SCORE_SLOTS]
    p_slots = refs[8 + N_SCORE_SLOTS:]

    step = pl.program_id(1)
    flags = flag_ref[step]
    first = (flags & STEP_FIRST) != 0
    last = (flags & STEP_LAST) != 0

    def tiles_of(k_ref, vt_ref, k2_ref, vt2_ref, b_ref, variant, keys):
        tiles = []
        for h in range(H_A):
            sl = slice(h * 2 * HD_A, (h + 1) * 2 * HD_A)
            vsl = slice(h * VA_PAD, (h + 1) * VA_PAD)
            bias = None if b_ref is None else (lambda h=h: b_ref[variant, h])
            for mp, q_ref in enumerate((qa1_ref, qa2_ref)):
                tiles.append((lambda sl=sl: k_ref[keys, sl], lambda sl=sl, q_ref=q_ref: q_ref[sl, :],
                              lambda vsl=vsl: vt_ref[vsl, keys], bias, acca, 2 * h + mp, 2 * h + mp))
        bias = None if b_ref is None else (lambda: b_ref[variant, H_A])
        for h in range(H_B):
            sl = slice(h * HB_PAD, (h + 1) * HB_PAD)
            vsl = slice(h * VB_PAD, (h + 1) * VB_PAD)
            tiles.append((lambda sl=sl: k2_ref[keys, sl], lambda sl=sl: qb_ref[sl, :],
                          lambda vsl=vsl: vt2_ref[vsl, keys], bias, accb, h, 2 * H_A + h))
        return tiles

    @pl.when(first)
    def _():
        m_ref[...] = jnp.full(m_ref.shape, MASKED, F32)
        acca[...] = jnp.zeros(acca.shape, F32)
        accb[...] = jnp.zeros(accb.shape, F32)
        if has_meta:
            for p_slot in p_slots:
                p_slot[0:LANES, :] = jnp.zeros((LANES, p_slot.shape[1]), BF16)
            tiles = tiles_of(mka_ref, mvat_ref, mkb_ref, mvbt_ref, mbias_ref, mvar_ref[step], slice(None))
            _softmax_tiles(tiles, N_META, LANES, s_slots, p_slots, m_ref, alpha_ref)

    for kind_id, classes in enumerate(kinds):
        @pl.when(kind_ref[step] == kind_id)
        def _(classes=classes):
            tiles = []
            for s, cls in enumerate(classes):
                if cls != SUB_SKIP:
                    tiles += tiles_of(ka_ref, vat_ref, kb_ref, vbt_ref, bias_ref if cls == SUB_NEAR else None,
                                      var_ref[step * MAX_SUB + s], slice(s * sub, (s + 1) * sub))
            _softmax_tiles(tiles, sub, sub, s_slots, p_slots, m_ref, alpha_ref)

    @pl.when(last)
    def _():
        al = lam_ref[...]
        lam = (jnp.exp(jnp.sum(al[0:1] * al[1:2], axis=1, keepdims=True))
               - jnp.exp(jnp.sum(al[2:3] * al[3:4], axis=1, keepdims=True)) + lam_init)
        sg = sg_ref[...]

        def normalised(acc_ref, idx, dv):
            a = acc_ref[idx]
            return a[0:dv] / a[dv:dv + 1]

        for h in range(H_A):
            o = normalised(acca, 2 * h, DV_A) - lam * normalised(acca, 2 * h + 1, DV_A)
            y = o * lax.rsqrt(jnp.mean(o * o, axis=0, keepdims=True) + EPS) * sg * (1.0 - lam_init)
            oa_ref[:, h * DV_A:(h + 1) * DV_A] = y.T.astype(BF16)
        for j in range(H_B // 2):
            pair = jnp.concatenate([normalised(accb, 2 * j, V_B), normalised(accb, 2 * j + 1, V_B)],
                                   axis=0)
            ob_ref[:, j * 2 * V_B:(j + 1) * 2 * V_B] = pair.T.astype(BF16)


def _t5_bucket(rel):
    half = NUM_BUCKETS // 2
    max_exact = half // 2
    n = jnp.abs(rel)
    nf = jnp.maximum(n, 1).astype(F32)
    large = max_exact + (jnp.log(nf / max_exact) / math.log(MAX_DISTANCE / max_exact)
                         * (half - max_exact)).astype(jnp.int32)
    large = jnp.minimum(large, half - 1)
    return jnp.where(rel > 0, half, 0) + jnp.where(n < max_exact, n, large)


def _bias_tile(table, q_pos, k_pos, k_valid):
    q_pos = jnp.asarray(q_pos, jnp.int32)
    k_pos = jnp.asarray(k_pos, jnp.int32)
    rel = k_pos[:, None] - q_pos[None, :]
    table = table.astype(F32)
    onehot = (_t5_bucket(rel)[..., None] == jnp.arange(NUM_BUCKETS)).astype(F32)
    shifted = (table - table[NUM_BUCKETS // 2 - 1]) * LOG2E
    bias = jnp.einsum("kqb,bh->hkq", onehot, shifted, precision=lax.Precision.HIGHEST)
    bias = jnp.concatenate([bias, jnp.zeros((1,) + rel.shape, F32)], axis=0)
    visible = ((k_pos[:, None] // CHUNK) <= (q_pos[None, :] // CHUNK)) & jnp.asarray(k_valid)[:, None]
    return jnp.where(visible[None], bias, MASKED)


def _plan_tiles(q_pos, k_pos, k_valid, tq, tk, sub, always_near):
    nq, nk, n_sub = len(q_pos) // tq, len(k_pos) // tk, tk // sub
    pairs, variants, defs = [], {}, []
    for qi in range(nq):
        qp = q_pos[qi * tq:(qi + 1) * tq]
        for ki in range(nk):
            classes, var = [], []
            for s in range(n_sub):
                rows = slice(ki * tk + s * sub, ki * tk + (s + 1) * sub)
                kp, kv = k_pos[rows], k_valid[rows]
                vis = ((kp[:, None] // CHUNK) <= (qp[None, :] // CHUNK)) & kv[:, None]
                rel = kp[:, None] - qp[None, :]
                if not (always_near or vis.any()):
                    classes.append(SUB_SKIP)
                    var.append(0)
                elif vis.all() and rel.max() <= -FAR_REL and not always_near:
                    classes.append(SUB_FAR)
                    var.append(0)
                else:
                    key = (int(kp[0] - qp[0]), int(qp[0] % CHUNK), int(kp[0] % CHUNK), int(kv.sum()),
                           tuple(np.diff(qp).tolist()) if np.any(np.diff(qp) != 1) else ())
                    if key not in variants:
                        variants[key] = len(defs)
                        defs.append((qp, kp, kv))
                    classes.append(SUB_NEAR)
                    var.append(variants[key])
            if any(c != SUB_SKIP for c in classes):
                pairs.append((qi, ki, tuple(classes), var))
    return pairs, defs


def _attention_plan(table, q_pos, k_pos, k_valid, tq, tk, sub, has_meta):
    n_sub = tk // sub
    assert tk % sub == 0 and n_sub <= MAX_SUB
    pairs, defs = _plan_tiles(q_pos, k_pos, k_valid, tq, tk, sub, always_near=False)
    if not defs:
        defs = [(q_pos[:tq], k_pos[:sub], k_valid[:sub])]
    bias = jnp.stack([_bias_tile(table, *d) for d in defs])
    n = len(pairs)
    qi = np.array([p[0] for p in pairs], np.int32)
    ki = np.array([p[1] for p in pairs], np.int32)
    kinds = sorted(set(p[2] for p in pairs))
    kind = np.array([kinds.index(p[2]) for p in pairs], np.int32)
    var = np.zeros((n, MAX_SUB), np.int32)
    for s, p in enumerate(pairs):
        var[s, :n_sub] = p[3]
    flags = np.zeros(n, np.int32)
    for s in range(n):
        if s == 0 or qi[s] != qi[s - 1]:
            flags[s] |= STEP_FIRST
        if s == n - 1 or qi[s] != qi[s + 1]:
            flags[s] |= STEP_LAST
    assert sorted(set(qi.tolist())) == list(range(len(q_pos) // tq))
    plan = dict(tq=tq, tk=tk, sub=sub, kinds=tuple(kinds), qi=qi, ki=ki, var=var.reshape(-1), flags=flags,
                kind=kind, bias=bias, mvar=np.zeros(n, np.int32), mbias=None)
    if has_meta:
        mpairs, mdefs = _plan_tiles(q_pos, np.arange(-N_META, 0), np.ones(N_META, bool), tq, N_META, N_META,
                                    always_near=True)
        plan["mbias"] = jnp.stack([_bias_tile(table, *d) for d in mdefs])
        plan["mvar"] = np.array([p[3][0] for p in mpairs], np.int32)[qi]
    return plan


def _attention(p, meta, plan, a_lambda, subln_g, lam_init):
    b, da, lq = p["qa1"].shape
    hv = H_B * V_B
    tq, tk, sub = plan["tq"], plan["tk"], plan["sub"]
    n_steps = len(plan["qi"])
    has_meta = meta is not None

    qblk = lambda w: pl.BlockSpec((None, tq, w), lambda i, s, qi_, *_: (i, qi_[s], 0))
    qtblk = lambda w: pl.BlockSpec((None, w, tq), lambda i, s, qi_, *_: (i, 0, qi_[s]))
    kblk = lambda w: pl.BlockSpec((None, tk, w), lambda i, s, qi_, ki_, *_: (i, ki_[s], 0))
    ktblk = lambda w: pl.BlockSpec((None, w, tk), lambda i, s, qi_, ki_, *_: (i, 0, ki_[s]))
    slot_rows = max(sub, LANES)
    in_specs = [qtblk(da), qtblk(da), kblk(da), ktblk(H_A * VA_PAD), qtblk(H_B * HB_PAD), kblk(H_B * HB_PAD),
                ktblk(H_B * VB_PAD), _const_spec(plan["bias"].shape)]
    args = [p["qa1"], p["qa2"], p["ka"], p["vat"], p["qb"], p["kb"], p["vbt"], plan["bias"]]
    if has_meta:
        margs = [meta["ka"], meta["vat"], meta["kb"], meta["vbt"], plan["mbias"]]
        in_specs += [_const_spec(a.shape) for a in margs]
        args += margs
    in_specs += [_const_spec(a_lambda.shape), _const_spec((DV_A, 1))]
    args += [a_lambda, subln_g.reshape(DV_A, 1)]
    grid_spec = pltpu.PrefetchScalarGridSpec(
        num_scalar_prefetch=6, grid=(b, n_steps), in_specs=in_specs,
        out_specs=[qblk(da), qblk(hv)],
        scratch_shapes=[pltpu.VMEM((N_TILES, SUBLANES, tq), F32),
                        pltpu.VMEM((MAX_SUB * N_TILES, SUBLANES, tq), F32),
                        pltpu.VMEM((2 * H_A, VA_PAD, tq), F32), pltpu.VMEM((H_B, VB_PAD, tq), F32)]
        + [pltpu.VMEM((slot_rows, tq), F32)] * N_SCORE_SLOTS
        + [pltpu.VMEM((slot_rows, tq), BF16)] * N_PROB_SLOTS)
    oa, ob = pl.pallas_call(
        functools.partial(_attn_kernel, has_meta=has_meta, lam_init=lam_init, sub=sub, kinds=plan["kinds"]),
        grid_spec=grid_spec,
        out_shape=[jax.ShapeDtypeStruct((b, lq, da), BF16), jax.ShapeDtypeStruct((b, lq, hv), BF16)],
        compiler_params=pltpu.CompilerParams(dimension_semantics=("parallel", "arbitrary"),
                                             vmem_limit_bytes=VMEM_LIMIT),
    )(*(jnp.asarray(plan[k]) for k in ("qi", "ki", "var", "mvar", "flags", "kind")), *args)
    return oa, ob


DOWN_GROUP = 4


def _post_kernel(x_ref, oa_ref, ob_ref, gmix_ref, wg_ref, wabr_ref, wbbr_ref, wout_ref, gffn_ref, wup_ref,
                 cw_ref, cb_ref, wdn_ref, gfin_ref, cbuf_ref, h_ref, cst_ref, acc_ref, ext0_ref, ext1_ref,
                 carry_ref, act_ref, *, tm, fc, n_chunks, final):
    j = pl.program_id(1)
    d = x_ref.shape[-1]
    x = x_ref[...]
    hb = _rms(x, gmix_ref[...]).astype(BF16)
    gates = jax.nn.sigmoid(_dot(hb, wg_ref[...]))
    m = gates[:, :d] * _dot(oa_ref[...], wabr_ref[...]) + gates[:, d:] * _dot(ob_ref[...], wbbr_ref[...])
    h = x + _dot(m.astype(BF16), wout_ref[...])
    hb2 = _rms(h, gffn_ref[...]).astype(BF16)

    @pl.when(j == 0)
    def _():
        carry_ref[...] = cbuf_ref[...]

    acc_ref[...] = h
    ext_refs = (ext0_ref, ext1_ref)
    chunk_cols = lambda c: slice(c * 2 * fc, (c + 1) * 2 * fc)

    def up_project(c):
        ext_refs[c % 2][SUBLANES:SUBLANES + tm, :] = _dot(hb2, wup_ref[:, chunk_cols(c)])

    up_project(0)
    for c in range(n_chunks):
        cols = chunk_cols(c)
        ext_ref = ext_refs[c % 2]
        if c + 1 < n_chunks:
            up_project(c + 1)
        ext_ref[0:SUBLANES, :] = carry_ref[:, cols]
        u = ext_ref[SUBLANES:SUBLANES + tm, :]
        cw = cw_ref[:, cols]
        conv = (cb_ref[:, cols] + cw[0:1] * ext_ref[SUBLANES - 2:SUBLANES - 2 + tm, :]
                + cw[1:2] * ext_ref[SUBLANES - 1:SUBLANES - 1 + tm, :] + cw[2:3] * u)
        carry_ref[:, cols] = ext_ref[tm:tm + SUBLANES, :]
        a_up = conv[:, :fc]
        g_up = conv[:, fc:]
        act_ref[:, c * fc:(c + 1) * fc] = (g_up * jax.nn.sigmoid(g_up) * a_up).astype(BF16)
        if (c + 1) % DOWN_GROUP == 0 or c + 1 == n_chunks:
            lo = (c // DOWN_GROUP) * DOWN_GROUP * fc
            acc_ref[...] += _dot(act_ref[:, lo:(c + 1) * fc], wdn_ref[lo:(c + 1) * fc, :])
    h_ref[...] = _rms(acc_ref[...], gfin_ref[...]) if final else acc_ref[...]

    @pl.when(j == pl.num_programs(1) - 1)
    def _():
        cst_ref[...] = carry_ref[...]


def _post(x, oa, ob, cbuf8, lw, tm, final_g, final):
    b, L, d = x.shape
    ff2 = lw["wup"].shape[1]
    fc = lw["fc"]
    n_chunks = ff2 // (2 * fc)
    row = lambda w: pl.BlockSpec((None, tm, w), lambda i, j: (i, j, 0))
    st = pl.BlockSpec((None, SUBLANES, ff2), lambda i, j: (i, 0, 0))
    weights = [lw["g_mix"], lw["wg"], lw["wabr"], lw["wbbr"], lw["wout"], lw["g_ffn"], lw["wup"],
               lw["conv_w"], lw["conv_b"], lw["wdn"], final_g.reshape(1, d)]
    return pl.pallas_call(
        functools.partial(_post_kernel, tm=tm, fc=fc, n_chunks=n_chunks, final=final),
        grid=(b, L // tm),
        in_specs=[row(d), row(oa.shape[-1]), row(ob.shape[-1])] + [_const_spec(w.shape) for w in weights] + [st],
        out_specs=[row(d), st],
        out_shape=[jax.ShapeDtypeStruct((b, L, d), F32), jax.ShapeDtypeStruct((b, SUBLANES, ff2), F32)],
        scratch_shapes=[pltpu.VMEM((tm, d), F32), pltpu.VMEM((tm + SUBLANES, 2 * fc), F32),
                        pltpu.VMEM((tm + SUBLANES, 2 * fc), F32), pltpu.VMEM((SUBLANES, ff2), F32),
                        pltpu.VMEM((tm, ff2 // 2), BF16)],
        compiler_params=pltpu.CompilerParams(dimension_semantics=("parallel", "arbitrary"),
                                             vmem_limit_bytes=VMEM_LIMIT),
    )(x, oa, ob, *weights, cbuf8)


def _layer_weights(l, norm_mix_g, w_in, b_q_norm_g, w_b_uq, b_kv_norm_g, w_b_ukv, w_a_br, w_b_br, w_out,
                   norm_ffn_g, w_up, conv_w, conv_b, w_down):
    d = w_in.shape[1]
    da = H_A * 2 * HD_A
    q_lora = w_b_uq.shape[1]
    kvl = w_b_ukv.shape[1]
    d_ff = w_down.shape[1]
    w = w_in[l]
    o_cq = 3 * da
    o_ckv = o_cq + q_lora
    o_kr = o_ckv + kvl
    o_g = o_kr + ROPE_B
    bf = lambda a: a.astype(BF16)
    wv = w[:, 2 * da:3 * da]
    wkr = jnp.zeros((d, LANES), F32).at[:, ROPE_LO:ROPE_LO + ROPE_B].set(w[:, o_kr:o_g])
    wuq = jnp.pad(w_b_uq[l], ((0, 0), (0, 0), (0, HB_PAD - NOPE_B - ROPE_B))).reshape(q_lora, H_B * HB_PAD)
    wuk = jnp.pad(w_b_ukv[l][:, :, :NOPE_B], ((0, 0), (0, 0), (0, HB_PAD - NOPE_B))).reshape(kvl, H_B * HB_PAD)
    wuv = w_b_ukv[l][:, :, NOPE_B:].reshape(kvl, H_B * V_B)
    fc = 256 if d_ff % 256 == 0 else d_ff
    nch = d_ff // fc
    regroup = lambda a: a.reshape(a.shape[0], 2, nch, fc).transpose(0, 2, 1, 3).reshape(a.shape[0], 2 * d_ff)
    return dict(
        g_mix=norm_mix_g[l].reshape(1, d), wq=bf(w[:, :da]), wq_t=bf(w[:, :da].T), wk=bf(w[:, da:2 * da]),
        wv=bf(wv),
        wcq=bf(w[:, o_cq:o_ckv]), wckv=bf(w[:, o_ckv:o_kr]), wkr=bf(wkr), wg=bf(w[:, o_g:]),
        g_q=b_q_norm_g[l].reshape(1, q_lora), wuq=bf(wuq), wuq_t=bf(wuq.T),
        g_kv=b_kv_norm_g[l].reshape(1, kvl), wuk=bf(wuk),
        wuv=bf(wuv), wuv_t=bf(wuv.T), wabr=bf(w_a_br[l]), wbbr=bf(w_b_br[l]), wout=bf(w_out[l]),
        g_ffn=norm_ffn_g[l].reshape(1, d), wup=bf(regroup(w_up[l])),
        conv_w=jnp.pad(regroup(conv_w[l]), ((0, SUBLANES - CONV_W), (0, 0))),
        conv_b=regroup(conv_b[l].reshape(1, 2 * d_ff)), wdn=bf(w_down[l]), fc=fc, nch=nch, d_ff=d_ff)


def _ungroup_conv_state(cst, lw):
    b = cst.shape[0]
    s = cst[:, SUBLANES - (CONV_W - 1):, :].reshape(b, CONV_W - 1, lw["nch"], 2, lw["fc"])
    return s.transpose(0, 1, 3, 2, 4).reshape(b, CONV_W - 1, 2 * lw["d_ff"])


def _group_conv_state(buf, lw):
    b = buf.shape[0]
    s = buf.reshape(b, CONV_W - 1, 2, lw["nch"], lw["fc"]).transpose(0, 1, 3, 2, 4)
    s = s.reshape(b, CONV_W - 1, 2 * lw["d_ff"])
    return jnp.pad(s, ((0, 0), (SUBLANES - (CONV_W - 1), 0), (0, 0)))


def _rope_tables(pos):
    inv = ROPE_THETA ** (-jnp.arange(ROPE_HALF, dtype=F32) / ROPE_HALF)
    ang = jnp.asarray(pos).astype(F32)[:, None] * inv[None, :]
    c, s = jnp.cos(ang), jnp.sin(ang)
    z = jnp.zeros_like(c)
    n = pos.shape[0]
    tail = jnp.zeros((n, LANES - ROPE_LO - ROPE_B), F32)
    head0 = jnp.zeros((n, ROPE_LO), F32)
    cos = jnp.concatenate([jnp.ones((n, ROPE_LO), F32), c, c, tail], axis=1)
    sin_lo = jnp.concatenate([head0, -s, z, tail], axis=1)
    sin_hi = jnp.concatenate([head0, z, s, tail], axis=1)
    return cos, sin_lo, sin_hi, c.T, s.T


def _pad_rows(a, n, axis):
    pad = [(0, 0)] * a.ndim
    pad[axis] = (0, n - a.shape[axis])
    return jnp.pad(a, pad)


def _feature_major_values(v, heads, dv):
    b, rows, _ = v.shape
    vt = jnp.swapaxes(v, 1, 2).reshape(b, heads, dv, rows)
    extra = jnp.zeros((b, heads, BF16_ROWS, rows), BF16).at[:, :, 0, :].set(1.0)
    return jnp.concatenate([vt, extra], axis=2).reshape(b, heads * (dv + BF16_ROWS), rows)


def _row_tile(L, want):
    t = min(L, want)
    while L % t:
        t //= 2
    return t


def kernel(x_prompt, x_sample, cache_a_k, cache_a_v, cache_b_ckv, cache_b_krope, state_ffn_conv, meta_tokens,
           rel_bias_table, norm_mix_g, w_in, a_lambda, a_subln_g, b_q_norm_g, w_b_uq, b_kv_norm_g, w_b_ukv,
           w_a_br, w_b_br, w_out, norm_ffn_g, w_up, conv_w, conv_b, w_down, final_norm_g):
    bp, seq, d = x_prompt.shape
    bs, dec, _ = x_sample.shape
    depth = w_in.shape[0]
    past = cache_a_k.shape[2]
    da = H_A * 2 * HD_A
    ff2 = w_up.shape[2]
    assert seq % CHUNK == 0 and N_META % SUBLANES == 0 and dec % SUBLANES == 0

    meta_pos = np.arange(-N_META, 0)
    prompt_pos = np.arange(seq)
    sample_pos = np.arange(past, past + dec)
    tabs_m, tabs_p, tabs_s = _rope_tables(meta_pos), _rope_tables(prompt_pos), _rope_tables(sample_pos)

    tm_p = _row_tile(seq, 256)
    tq_p = sub_p = _row_tile(seq, 256)
    tk_p = MAX_SUB * sub_p if seq % (MAX_SUB * sub_p) == 0 else sub_p
    qpad = lambda a: _pad_rows(jnp.swapaxes(a, 1, 2), LANES, 2)
    mq_pos = np.concatenate([meta_pos, np.full(LANES - N_META, -1)])
    mk_valid = np.arange(LANES) < N_META
    sq_pos = np.concatenate([sample_pos, np.full(LANES - dec, sample_pos[-1])])
    lk_s = -(-(past + dec) // LANES) * LANES
    sk_pos = np.concatenate([np.arange(past + dec), np.zeros(lk_s - past - dec, np.int64)])
    sk_valid = np.arange(lk_s) < past + dec
    tm_c = _row_tile(past, 256)
    plan_m = _attention_plan(rel_bias_table, mq_pos, mq_pos, mk_valid, LANES, LANES, LANES, has_meta=False)
    plan_p = _attention_plan(rel_bias_table, prompt_pos, prompt_pos, np.ones(seq, bool), tq_p, tk_p, sub_p,
                             has_meta=True)
    plan_s = _attention_plan(rel_bias_table, sq_pos, sk_pos, sk_valid, LANES, lk_s, lk_s, has_meta=True)

    h_m = meta_tokens[None].astype(F32)
    h_p, h_s = x_prompt, x_sample
    outs = {k: [] for k in ("ak_s", "av_s", "ck_s", "kr_s", "cv_p", "cv_s")}
    meta_rows = {name: [] for name in CACHE_NAMES}
    cache_p = None
    for l in range(depth):
        lam_init = 0.8 - 0.6 * math.exp(-0.3 * l)
        lw = _layer_weights(l, norm_mix_g, w_in, b_q_norm_g, w_b_uq, b_kv_norm_g, w_b_ukv, w_a_br, w_b_br,
                            w_out, norm_ffn_g, w_up, conv_w, conv_b, w_down)
        attn = functools.partial(_attention, a_lambda=a_lambda[l], subln_g=a_subln_g[l], lam_init=lam_init)

        pm = _projection(h_m, lw, tabs_m, N_META, transposed=False)
        meta = dict(ka=pm["k16"][0], vat=_pad_rows(_feature_major_values(pm["v16"], H_A, DV_A)[0], LANES, 1),
                    kb=pm["kb"][0], vbt=_pad_rows(_feature_major_values(pm["vb"], H_B, V_B)[0], LANES, 1))
        qm = dict(qa1=qpad(pm["qa1"]), qa2=qpad(pm["qa2"]), qb=qpad(pm["qb"]),
                  ka=_pad_rows(meta["ka"], LANES, 0)[None], vat=meta["vat"][None],
                  kb=_pad_rows(meta["kb"], LANES, 0)[None], vbt=meta["vbt"][None])
        oa, ob = attn(qm, None, plan_m)
        post = functools.partial(_post, final_g=final_norm_g)
        h_m, cst_m = post(h_m, oa[:, :N_META], ob[:, :N_META], jnp.zeros((1, SUBLANES, ff2), F32), lw, N_META,
                          final=False)

        pp = _projection(h_p, lw, tabs_p, tm_p, transposed=True, cache=(depth, l, N_META, cache_p))
        cache_p = {name: pp[name] for name in CACHE_NAMES}
        qp = dict(qa1=pp["qa1"], qa2=pp["qa2"], qb=pp["qb"], ka=pp["k16"], vat=pp["v16"], kb=pp["kb"],
                  vbt=pp["vb"])
        oa, ob = attn(qp, meta, plan_p)
        last = l == depth - 1
        h_p, cst_p = post(h_p, oa, ob, jnp.broadcast_to(cst_m, (bp, SUBLANES, ff2)), lw, tm_p, final=last)

        ps = _projection(h_s, lw, tabs_s, dec, transposed=False)
        kr_cache = jnp.pad(cache_b_krope[l], ((0, 0), (0, 0), (ROPE_LO, LANES - ROPE_LO - ROPE_B)))
        kb_c, vbt_c = _kv_up_cache(cache_b_ckv[l], kr_cache, lw, tm_c)
        va_all = jnp.concatenate([cache_a_v[l].reshape(bs, past, da).astype(BF16), ps["v16"]], axis=1)
        vb_all = jnp.concatenate([jnp.swapaxes(vbt_c, 1, 2), ps["vb"]], axis=1)
        qs = dict(qa1=qpad(ps["qa1"]), qa2=qpad(ps["qa2"]), qb=qpad(ps["qb"]),
                  ka=_pad_rows(jnp.concatenate([cache_a_k[l].reshape(bs, past, da).astype(BF16), ps["k16"]],
                                               axis=1), lk_s, 1),
                  vat=_pad_rows(_feature_major_values(va_all, H_A, DV_A), lk_s, 2),
                  kb=_pad_rows(jnp.concatenate([kb_c, ps["kb"]], axis=1), lk_s, 1),
                  vbt=_pad_rows(_feature_major_values(vb_all, H_B, V_B), lk_s, 2))
        oa, ob = attn(qs, meta, plan_s)
        h_s, cst_s = post(h_s, oa[:, :dec], ob[:, :dec], _group_conv_state(state_ffn_conv[l], lw), lw, dec,
                          final=last)

        heads = lambda a: a.reshape(a.shape[:-1] + (H_A, 2 * HD_A))
        for name in CACHE_NAMES:
            meta_rows[name].append(pm[name])
        outs["cv_p"].append(_ungroup_conv_state(cst_p, lw))
        outs["ak_s"].append(heads(ps["kf"]))
        outs["av_s"].append(heads(ps["vf"]))
        outs["ck_s"].append(ps["ckv"])
        outs["kr_s"].append(ps["kr"])
        outs["cv_s"].append(_ungroup_conv_state(cst_s, lw))

    y_p, y_s = h_p, h_s
    st = lambda k: jnp.stack(outs[k])

    def with_meta_rows(name):
        rows = jnp.stack(meta_rows[name])
        rows = jnp.broadcast_to(rows, (depth, bp) + rows.shape[2:])
        return cache_p[name].at[:, :, :N_META].set(rows)

    return (y_p, y_s, heads(with_meta_rows("kf")), st("ak_s"), heads(with_meta_rows("vf")), st("av_s"),
            with_meta_rows("ckv"), st("ck_s"), with_meta_rows("kr"), st("kr_s"), st("cv_p"), st("cv_s"))
```

```python
import functools
import math

import numpy as np
import jax
import jax.numpy as jnp
from jax import lax
from jax.experimental import pallas as pl
from jax.experimental.pallas import tpu as pltpu

CHUNK = 64
N_META = 16
EPS = 1e-6
H_A = 8
HD_A = 64
H_B = 8
NOPE_B = 64
ROPE_B = 32
V_B = 64
ROPE_THETA = 10000.0
NUM_BUCKETS = 32
MAX_DISTANCE = 128
CONV_W = 3

LANES = 128
SUBLANES = 8
BF16_ROWS = 16
LOG2E = 1.4426950408889634
MASKED = -1e30
FAR_REL = 91
VMEM_LIMIT = 56 * 1024 * 1024

F32 = jnp.float32
BF16 = jnp.bfloat16
HB_PAD = LANES
ROPE_LO = NOPE_B
ROPE_HALF = ROPE_B // 2
DV_A = 2 * HD_A
VA_PAD = DV_A + BF16_ROWS
VB_PAD = V_B + BF16_ROWS


def _dot(a, b):
    return jnp.dot(a, b, preferred_element_type=F32)


def _dot_nt(a, b):
    return lax.dot_general(a, b, (((1,), (1,)), ((), ())), preferred_element_type=F32)


def _rms(x, g):
    return x * lax.rsqrt(jnp.mean(x * x, axis=-1, keepdims=True) + EPS) * g


def _rope_padded(x, cos, sin_lo, sin_hi):
    return (x * cos + pltpu.roll(x, LANES - ROPE_HALF, 1) * sin_lo
            + pltpu.roll(x, ROPE_HALF, 1) * sin_hi)


def _const_spec(shape):
    nd = len(shape)
    return pl.BlockSpec(shape, lambda *_: (0,) * nd, pipeline_mode=pl.Buffered(1))


def _sum_rows(n):
    r = lax.broadcasted_iota(jnp.int32, (BF16_ROWS, n), 0)
    return jnp.where(r == 0, 1.0, 0.0).astype(BF16)


N_PROJ_INPUTS = 18


def _proj_kernel(*refs, transposed, n_alias):
    (x_ref, g_ref, wq_ref, wk_ref, wv_ref, wcq_ref, wckv_ref, wkr_ref, qng_ref,
     wuq_ref, kvng_ref, wuk_ref, wuv_ref, cos_ref, slo_ref, shi_ref, cost_ref, sint_ref) = refs[:N_PROJ_INPUTS]
    (qa1_ref, qa2_ref, kf_ref, k16_ref, vf_ref, v16_ref, qb_ref, ckv_ref, kr_ref, kb_ref,
     vb_ref) = refs[N_PROJ_INPUTS + n_alias:]
    x = x_ref[...]
    rows = x.shape[0]
    hb = _rms(x, g_ref[...]).astype(BF16)
    scale_a = HD_A ** -0.5 * LOG2E
    scale_b = (NOPE_B + ROPE_B) ** -0.5 * LOG2E
    cos = cos_ref[...]
    slo = slo_ref[...]
    shi = shi_ref[...]

    def store_rows(ref, val):
        ref[(0,) * (len(ref.shape) - 2)] = val

    v = _dot(hb, wv_ref[...])
    store_rows(vf_ref, v)
    if transposed:
        vt = v.T.astype(BF16)
        ones = _sum_rows(rows)
        for h in range(H_A):
            v16_ref[h * VA_PAD:h * VA_PAD + DV_A, :] = vt[h * DV_A:(h + 1) * DV_A]
            v16_ref[h * VA_PAD + DV_A:(h + 1) * VA_PAD, :] = ones
    else:
        v16_ref[...] = v.astype(BF16)
    k = _dot(hb, wk_ref[...])
    store_rows(kf_ref, k)
    k16_ref[...] = k.astype(BF16)

    if transposed:
        qa = _dot_nt(wq_ref[...], hb) * scale_a
        feat = lax.broadcasted_iota(jnp.int32, qa.shape, 0) % (2 * HD_A)
    else:
        qa = _dot(hb, wq_ref[...]) * scale_a
        feat = lax.broadcasted_iota(jnp.int32, qa.shape, 1) % (2 * HD_A)
    qa1_ref[...] = jnp.where(feat < HD_A, qa, 0.0).astype(BF16)
    qa2_ref[...] = jnp.where(feat >= HD_A, qa, 0.0).astype(BF16)

    cq = _rms(_dot(hb, wcq_ref[...]), qng_ref[...]).astype(BF16)
    if transposed:
        qb = _dot_nt(wuq_ref[...], cq) * scale_b
        ct = cost_ref[...]
        st = sint_ref[...]
        for h in range(H_B):
            lo = h * HB_PAD + ROPE_LO
            x1 = qb[lo:lo + ROPE_HALF]
            x2 = qb[lo + ROPE_HALF:lo + ROPE_B]
            qb_ref[h * HB_PAD:lo, :] = qb[h * HB_PAD:lo].astype(BF16)
            qb_ref[lo:lo + ROPE_HALF, :] = (x1 * ct - x2 * st).astype(BF16)
            qb_ref[lo + ROPE_HALF:lo + ROPE_B, :] = (x1 * st + x2 * ct).astype(BF16)
            qb_ref[lo + ROPE_B:(h + 1) * HB_PAD, :] = qb[lo + ROPE_B:(h + 1) * HB_PAD].astype(BF16)
    else:
        qb = _dot(cq, wuq_ref[...]) * scale_b
        for h in range(H_B):
            sl = slice(h * HB_PAD, (h + 1) * HB_PAD)
            qb_ref[:, sl] = _rope_padded(qb[:, sl], cos, slo, shi).astype(BF16)

    ckv = _rms(_dot(hb, wckv_ref[...]), kvng_ref[...])
    store_rows(ckv_ref, ckv)
    kr = _rope_padded(_dot(hb, wkr_ref[...]), cos, slo, shi)
    store_rows(kr_ref, kr[:, ROPE_LO:ROPE_LO + ROPE_B])
    cb = ckv.astype(BF16)
    kn = _dot(cb, wuk_ref[...])
    for h in range(H_B):
        sl = slice(h * HB_PAD, (h + 1) * HB_PAD)
        kb_ref[:, sl] = (kn[:, sl] + kr).astype(BF16)
    if transposed:
        vbt = _dot_nt(wuv_ref[...], cb).astype(BF16)
        ones = _sum_rows(rows)
        for h in range(H_B):
            vb_ref[h * VB_PAD:h * VB_PAD + V_B, :] = vbt[h * V_B:(h + 1) * V_B]
            vb_ref[h * VB_PAD + V_B:(h + 1) * VB_PAD, :] = ones
    else:
        vb_ref[...] = _dot(cb, wuv_ref[...]).astype(BF16)


CACHE_NAMES = ("kf", "vf", "ckv", "kr")


def _projection(x, lw, tabs, tm, transposed, cache=None):
    b, L, d = x.shape
    da = lw["wq"].shape[1]
    kvl = lw["wckv"].shape[1]
    hv = H_B * V_B
    hq = H_B * HB_PAD
    nj = L // tm
    row = lambda w: pl.BlockSpec((None, tm, w), lambda i, j: (i, j, 0))
    col = lambda w: pl.BlockSpec((None, w, tm), lambda i, j: (i, 0, j))
    tab = pl.BlockSpec((tm, LANES), lambda i, j: (j, 0))
    tab_t = pl.BlockSpec((ROPE_HALF, tm), lambda i, j: (0, j))
    t = "_t" if transposed else ""
    weights = [lw["g_mix"], lw["wq" + t], lw["wk"], lw["wv"], lw["wcq"], lw["wckv"], lw["wkr"],
               lw["g_q"], lw["wuq" + t], lw["g_kv"], lw["wuk"], lw["wuv" + t]]
    in_specs = [row(d)] + [_const_spec(w.shape) for w in weights] + [tab, tab, tab, tab_t, tab_t]
    sds = jax.ShapeDtypeStruct
    if transposed:
        feat = lambda w, wt: (sds((b, wt, L), BF16), col(wt))
    else:
        feat = lambda w, wt: (sds((b, L, w), BF16), row(w))
    if cache is None:
        f32row = lambda w: (sds((b, L, w), F32), row(w))
    else:
        depth, layer, lead, bufs = cache
        f32row = lambda w: (sds((depth, b, lead + L, w), F32),
                            pl.BlockSpec((pl.Element(1), pl.Element(1), pl.Element(tm), pl.Element(w)),
                                         lambda i, j: (layer, i, pl.multiple_of(lead + j * tm, SUBLANES), 0)))
    outs = [feat(da, da), feat(da, da), f32row(da), (sds((b, L, da), BF16), row(da)),
            f32row(da), feat(da, H_A * VA_PAD), feat(hq, hq), f32row(kvl), f32row(ROPE_B),
            (sds((b, L, hq), BF16), row(hq)), feat(hv, H_B * VB_PAD)]
    names = ("qa1", "qa2", "kf", "k16", "vf", "v16", "qb", "ckv", "kr", "kb", "vb")
    args = [x, *weights, *tabs]
    aliases = {}
    if cache is not None and bufs is not None:
        for name in CACHE_NAMES:
            aliases[len(args)] = names.index(name)
            args.append(bufs[name])
            in_specs.append(pl.BlockSpec(memory_space=pl.ANY))
    res = pl.pallas_call(
        functools.partial(_proj_kernel, transposed=transposed, n_alias=len(aliases)),
        grid=(b, nj), in_specs=in_specs, out_specs=[s for _, s in outs], out_shape=[s for s, _ in outs],
        input_output_aliases=aliases,
        compiler_params=pltpu.CompilerParams(dimension_semantics=("parallel", "parallel"),
                                             vmem_limit_bytes=VMEM_LIMIT),
    )(*args)
    return dict(zip(names, res))


def _kvup_kernel(ckv_ref, kr_ref, wuk_ref, wuvt_ref, kb_ref, vbt_ref):
    cb = ckv_ref[...].astype(BF16)
    kn = _dot(cb, wuk_ref[...])
    kr = kr_ref[...]
    for h in range(H_B):
        sl = slice(h * HB_PAD, (h + 1) * HB_PAD)
        kb_ref[:, sl] = (kn[:, sl] + kr).astype(BF16)
    vbt_ref[...] = _dot_nt(wuvt_ref[...], cb).astype(BF16)


def _kv_up_cache(ckv, kr_pad, lw, tm):
    b, p, kvl = ckv.shape
    hv = H_B * V_B
    row = lambda w: pl.BlockSpec((None, tm, w), lambda i, j: (i, j, 0))
    return pl.pallas_call(
        _kvup_kernel, grid=(b, p // tm),
        in_specs=[row(kvl), row(LANES), _const_spec(lw["wuk"].shape), _const_spec(lw["wuv_t"].shape)],
        out_specs=[row(H_B * HB_PAD), pl.BlockSpec((None, hv, tm), lambda i, j: (i, 0, j))],
        out_shape=[jax.ShapeDtypeStruct((b, p, H_B * HB_PAD), BF16), jax.ShapeDtypeStruct((b, hv, p), BF16)],
        compiler_params=pltpu.CompilerParams(dimension_semantics=("parallel", "parallel"),
                                             vmem_limit_bytes=VMEM_LIMIT),
    )(ckv, kr_pad, lw["wuk"], lw["wuv_t"])


N_TILES = 2 * H_A + H_B
SCORE_LAG = 6
PV_LAG = 3
N_SCORE_SLOTS = SCORE_LAG + 2
N_PROB_SLOTS = PV_LAG + 2
MAX_SUB = 2
STEP_FIRST, STEP_LAST = 1, 2
SUB_SKIP, SUB_FAR, SUB_NEAR = "skip", "far", "near"


def _softmax_tiles(tiles, tk, tk_values, s_slots, p_slots, m_ref, alpha_ref):
    n = len(tiles)
    assert n <= alpha_ref.shape[0] and max(SCORE_LAG, PV_LAG) < N_TILES

    def by_sublanes(x):
        return x.reshape(x.shape[0] // SUBLANES, SUBLANES, x.shape[1])

    def scores(t):
        k, q_t, _, bias, _, _, stat = tiles[t]
        st = _dot(k(), q_t())
        if bias is not None:
            st = st + bias()
        s_slots[t % N_SCORE_SLOTS][0:tk, :] = st
        m_old = m_ref[stat]
        m_new = jnp.maximum(m_old, jnp.max(st, axis=0, keepdims=True))
        m_ref[stat] = m_new
        alpha_ref[t] = jnp.exp2(m_old - m_new)

    def probs(t):
        stat = tiles[t][6]
        s = by_sublanes(s_slots[t % N_SCORE_SLOTS][0:tk, :])
        p = jnp.exp2(s - m_ref[stat][None]).reshape(tk, s.shape[2])
        p_slots[t % N_PROB_SLOTS][0:tk, :] = p.astype(BF16)

    def values(t):
        _, _, v_t, _, acc_ref, idx, _ = tiles[t]
        pv = _dot(v_t(), p_slots[t % N_PROB_SLOTS][0:tk_values, :])
        acc = by_sublanes(acc_ref[idx]) * alpha_ref[t][None]
        acc_ref[idx] = acc.reshape(pv.shape) + pv

    for i in range(n + SCORE_LAG + PV_LAG):
        if i < n:
            scores(i)
        if 0 <= i - SCORE_LAG < n:
            probs(i - SCORE_LAG)
        if 0 <= i - SCORE_LAG - PV_LAG < n:
            values(i - SCORE_LAG - PV_LAG)


def _attn_kernel(qi_ref, ki_ref, var_ref, mvar_ref, flag_ref, kind_ref, *refs, has_meta, lam_init, sub, kinds):
    del qi_ref, ki_ref
    qa1_ref, qa2_ref, ka_ref, vat_ref, qb_ref, kb_ref, vbt_ref, bias_ref = refs[:8]
    refs = refs[8:]
    if has_meta:
        mka_ref, mvat_ref, mkb_ref, mvbt_ref, mbias_ref = refs[:5]
        refs = refs[5:]
    lam_ref, sg_ref, oa_ref, ob_ref, m_ref, alpha_ref, acca, accb = refs[:8]
    s_slots = refs[8:8 + N_SCORE_SLOTS]
    p_slots = refs[8 + N_SCORE_SLOTS:]

    step = pl.program_id(1)
    flags = flag_ref[step]
    first = (flags & STEP_FIRST) != 0
    last = (flags & STEP_LAST) != 0

    def tiles_of(k_ref, vt_ref, k2_ref, vt2_ref, b_ref, variant, keys):
        tiles = []
        for h in range(H_A):
            sl = slice(h * 2 * HD_A, (h + 1) * 2 * HD_A)
            vsl = slice(h * VA_PAD, (h + 1) * VA_PAD)
            bias = None if b_ref is None else (lambda h=h: b_ref[variant, h])
            for mp, q_ref in enumerate((qa1_ref, qa2_ref)):
                tiles.append((lambda sl=sl: k_ref[keys, sl], lambda sl=sl, q_ref=q_ref: q_ref[sl, :],
                              lambda vsl=vsl: vt_ref[vsl, keys], bias, acca, 2 * h + mp, 2 * h + mp))
        bias = None if b_ref is None else (lambda: b_ref[variant, H_A])
        for h in range(H_B):
            sl = slice(h * HB_PAD, (h + 1) * HB_PAD)
            vsl = slice(h * VB_PAD, (h + 1) * VB_PAD)
            tiles.append((lambda sl=sl: k2_ref[keys, sl], lambda sl=sl: qb_ref[sl, :],
                          lambda vsl=vsl: vt2_ref[vsl, keys], bias, accb, h, 2 * H_A + h))
        return tiles

    @pl.when(first)
    def _():
        m_ref[...] = jnp.full(m_ref.shape, MASKED, F32)
        acca[...] = jnp.zeros(acca.shape, F32)
        accb[...] = jnp.zeros(accb.shape, F32)
        if has_meta:
            for p_slot in p_slots:
                p_slot[0:LANES, :] = jnp.zeros((LANES, p_slot.shape[1]), BF16)
            tiles = tiles_of(mka_ref, mvat_ref, mkb_ref, mvbt_ref, mbias_ref, mvar_ref[step], slice(None))
            _softmax_tiles(tiles, N_META, LANES, s_slots, p_slots, m_ref, alpha_ref)

    for kind_id, classes in enumerate(kinds):
        @pl.when(kind_ref[step] == kind_id)
        def _(classes=classes):
            tiles = []
            for s, cls in enumerate(classes):
                if cls != SUB_SKIP:
                    tiles += tiles_of(ka_ref, vat_ref, kb_ref, vbt_ref, bias_ref if cls == SUB_NEAR else None,
                                      var_ref[step * MAX_SUB + s], slice(s * sub, (s + 1) * sub))
            _softmax_tiles(tiles, sub, sub, s_slots, p_slots, m_ref, alpha_ref)

    @pl.when(last)
    def _():
        al = lam_ref[...]
        lam = (jnp.exp(jnp.sum(al[0:1] * al[1:2], axis=1, keepdims=True))
               - jnp.exp(jnp.sum(al[2:3] * al[3:4], axis=1, keepdims=True)) + lam_init)
        sg = sg_ref[...]

        def normalised(acc_ref, idx, dv):
            a = acc_ref[idx]
            return a[0:dv] / a[dv:dv + 1]

        for h in range(H_A):
            o = normalised(acca, 2 * h, DV_A) - lam * normalised(acca, 2 * h + 1, DV_A)
            y = o * lax.rsqrt(jnp.mean(o * o, axis=0, keepdims=True) + EPS) * sg * (1.0 - lam_init)
            oa_ref[:, h * DV_A:(h + 1) * DV_A] = y.T.astype(BF16)
        for j in range(H_B // 2):
            pair = jnp.concatenate([normalised(accb, 2 * j, V_B), normalised(accb, 2 * j + 1, V_B)],
                                   axis=0)
            ob_ref[:, j * 2 * V_B:(j + 1) * 2 * V_B] = pair.T.astype(BF16)


def _t5_bucket(rel):
    half = NUM_BUCKETS // 2
    max_exact = half // 2
    n = jnp.abs(rel)
    nf = jnp.maximum(n, 1).astype(F32)
    large = max_exact + (jnp.log(nf / max_exact) / math.log(MAX_DISTANCE / max_exact)
                         * (half - max_exact)).astype(jnp.int32)
    large = jnp.minimum(large, half - 1)
    return jnp.where(rel > 0, half, 0) + jnp.where(n < max_exact, n, large)


def _bias_tile(table, q_pos, k_pos, k_valid):
    q_pos = jnp.asarray(q_pos, jnp.int32)
    k_pos = jnp.asarray(k_pos, jnp.int32)
    rel = k_pos[:, None] - q_pos[None, :]
    table = table.astype(F32)
    onehot = (_t5_bucket(rel)[..., None] == jnp.arange(NUM_BUCKETS)).astype(F32)
    shifted = (table - table[NUM_BUCKETS // 2 - 1]) * LOG2E
    bias = jnp.einsum("kqb,bh->hkq", onehot, shifted, precision=lax.Precision.HIGHEST)
    bias = jnp.concatenate([bias, jnp.zeros((1,) + rel.shape, F32)], axis=0)
    visible = ((k_pos[:, None] // CHUNK) <= (q_pos[None, :] // CHUNK)) & jnp.asarray(k_valid)[:, None]
    return jnp.where(visible[None], bias, MASKED)


def _plan_tiles(q_pos, k_pos, k_valid, tq, tk, sub, always_near):
    nq, nk, n_sub = len(q_pos) // tq, len(k_pos) // tk, tk // sub
    pairs, variants, defs = [], {}, []
    for qi in range(nq):
        qp = q_pos[qi * tq:(qi + 1) * tq]
        for ki in range(nk):
            classes, var = [], []
            for s in range(n_sub):
                rows = slice(ki * tk + s * sub, ki * tk + (s + 1) * sub)
                kp, kv = k_pos[rows], k_valid[rows]
                vis = ((kp[:, None] // CHUNK) <= (qp[None, :] // CHUNK)) & kv[:, None]
                rel = kp[:, None] - qp[None, :]
                if not (always_near or vis.any()):
                    classes.append(SUB_SKIP)
                    var.append(0)
                elif vis.all() and rel.max() <= -FAR_REL and not always_near:
                    classes.append(SUB_FAR)
                    var.append(0)
                else:
                    key = (int(kp[0] - qp[0]), int(qp[0] % CHUNK), int(kp[0] % CHUNK), int(kv.sum()),
                           tuple(np.diff(qp).tolist()) if np.any(np.diff(qp) != 1) else ())
                    if key not in variants:
                        variants[key] = len(defs)
                        defs.append((qp, kp, kv))
                    classes.append(SUB_NEAR)
                    var.append(variants[key])
            if any(c != SUB_SKIP for c in classes):
                pairs.append((qi, ki, tuple(classes), var))
    return pairs, defs


def _attention_plan(table, q_pos, k_pos, k_valid, tq, tk, sub, has_meta):
    n_sub = tk // sub
    assert tk % sub == 0 and n_sub <= MAX_SUB
    pairs, defs = _plan_tiles(q_pos, k_pos, k_valid, tq, tk, sub, always_near=False)
    if not defs:
        defs = [(q_pos[:tq], k_pos[:sub], k_valid[:sub])]
    bias = jnp.stack([_bias_tile(table, *d) for d in defs])
    n = len(pairs)
    qi = np.array([p[0] for p in pairs], np.int32)
    ki = np.array([p[1] for p in pairs], np.int32)
    kinds = sorted(set(p[2] for p in pairs))
    kind = np.array([kinds.index(p[2]) for p in pairs], np.int32)
    var = np.zeros((n, MAX_SUB), np.int32)
    for s, p in enumerate(pairs):
        var[s, :n_sub] = p[3]
    flags = np.zeros(n, np.int32)
    for s in range(n):
        if s == 0 or qi[s] != qi[s - 1]:
            flags[s] |= STEP_FIRST
        if s == n - 1 or qi[s] != qi[s + 1]:
            flags[s] |= STEP_LAST
    assert sorted(set(qi.tolist())) == list(range(len(q_pos) // tq))
    plan = dict(tq=tq, tk=tk, sub=sub, kinds=tuple(kinds), qi=qi, ki=ki, var=var.reshape(-1), flags=flags,
                kind=kind, bias=bias, mvar=np.zeros(n, np.int32), mbias=None)
    if has_meta:
        mpairs, mdefs = _plan_tiles(q_pos, np.arange(-N_META, 0), np.ones(N_META, bool), tq, N_META, N_META,
                                    always_near=True)
        plan["mbias"] = jnp.stack([_bias_tile(table, *d) for d in mdefs])
        plan["mvar"] = np.array([p[3][0] for p in mpairs], np.int32)[qi]
    return plan


def _attention(p, meta, plan, a_lambda, subln_g, lam_init):
    b, da, lq = p["qa1"].shape
    hv = H_B * V_B
    tq, tk, sub = plan["tq"], plan["tk"], plan["sub"]
    n_steps = len(plan["qi"])
    has_meta = meta is not None

    qblk = lambda w: pl.BlockSpec((None, tq, w), lambda i, s, qi_, *_: (i, qi_[s], 0))
    qtblk = lambda w: pl.BlockSpec((None, w, tq), lambda i, s, qi_, *_: (i, 0, qi_[s]))
    kblk = lambda w: pl.BlockSpec((None, tk, w), lambda i, s, qi_, ki_, *_: (i, ki_[s], 0))
    ktblk = lambda w: pl.BlockSpec((None, w, tk), lambda i, s, qi_, ki_, *_: (i, 0, ki_[s]))
    slot_rows = max(sub, LANES)
    in_specs = [qtblk(da), qtblk(da), kblk(da), ktblk(H_A * VA_PAD), qtblk(H_B * HB_PAD), kblk(H_B * HB_PAD),
                ktblk(H_B * VB_PAD), _const_spec(plan["bias"].shape)]
    args = [p["qa1"], p["qa2"], p["ka"], p["vat"], p["qb"], p["kb"], p["vbt"], plan["bias"]]
    if has_meta:
        margs = [meta["ka"], meta["vat"], meta["kb"], meta["vbt"], plan["mbias"]]
        in_specs += [_const_spec(a.shape) for a in margs]
        args += margs
    in_specs += [_const_spec(a_lambda.shape), _const_spec((DV_A, 1))]
    args += [a_lambda, subln_g.reshape(DV_A, 1)]
    grid_spec = pltpu.PrefetchScalarGridSpec(
        num_scalar_prefetch=6, grid=(b, n_steps), in_specs=in_specs,
        out_specs=[qblk(da), qblk(hv)],
        scratch_shapes=[pltpu.VMEM((N_TILES, SUBLANES, tq), F32),
                        pltpu.VMEM((MAX_SUB * N_TILES, SUBLANES, tq), F32),
                        pltpu.VMEM((2 * H_A, VA_PAD, tq), F32), pltpu.VMEM((H_B, VB_PAD, tq), F32)]
        + [pltpu.VMEM((slot_rows, tq), F32)] * N_SCORE_SLOTS
        + [pltpu.VMEM((slot_rows, tq), BF16)] * N_PROB_SLOTS)
    oa, ob = pl.pallas_call(
        functools.partial(_attn_kernel, has_meta=has_meta, lam_init=lam_init, sub=sub, kinds=plan["kinds"]),
        grid_spec=grid_spec,
        out_shape=[jax.ShapeDtypeStruct((b, lq, da), BF16), jax.ShapeDtypeStruct((b, lq, hv), BF16)],
        compiler_params=pltpu.CompilerParams(dimension_semantics=("parallel", "arbitrary"),
                                             vmem_limit_bytes=VMEM_LIMIT),
    )(*(jnp.asarray(plan[k]) for k in ("qi", "ki", "var", "mvar", "flags", "kind")), *args)
    return oa, ob


DOWN_GROUP = 4


def _post_kernel(x_ref, oa_ref, ob_ref, gmix_ref, wg_ref, wabr_ref, wbbr_ref, wout_ref, gffn_ref, wup_ref,
                 cw_ref, cb_ref, wdn_ref, gfin_ref, cbuf_ref, h_ref, cst_ref, acc_ref, ext0_ref, ext1_ref,
                 carry_ref, act_ref, *, tm, fc, n_chunks, final):
    j = pl.program_id(1)
    d = x_ref.shape[-1]
    x = x_ref[...]
    hb = _rms(x, gmix_ref[...]).astype(BF16)
    gates = jax.nn.sigmoid(_dot(hb, wg_ref[...]))
    m = gates[:, :d] * _dot(oa_ref[...], wabr_ref[...]) + gates[:, d:] * _dot(ob_ref[...], wbbr_ref[...])
    h = x + _dot(m.astype(BF16), wout_ref[...])
    hb2 = _rms(h, gffn_ref[...]).astype(BF16)

    @pl.when(j == 0)
    def _():
        carry_ref[...] = cbuf_ref[...]

    acc_ref[...] = h
    ext_refs = (ext0_ref, ext1_ref)
    d_ff = wdn_ref.shape[0]
    lanes = (slice(0, fc), slice(fc, 2 * fc))

    def halves(c):
        return slice(c * fc, (c + 1) * fc), slice(d_ff + c * fc, d_ff + (c + 1) * fc)

    def up_project(c):
        for cols, ln in zip(halves(c), lanes):
            ext_refs[c % 2][SUBLANES:SUBLANES + tm, ln] = _dot(hb2, wup_ref[:, cols])

    up_project(0)
    for c in range(n_chunks):
        ext_ref = ext_refs[c % 2]
        if c + 1 < n_chunks:
            up_project(c + 1)

        def conv_half(cols, ln):
            ext_ref[0:SUBLANES, ln] = carry_ref[:, cols]
            cw = cw_ref[:, cols]
            out = (cb_ref[:, cols] + cw[0:1] * ext_ref[SUBLANES - 2:SUBLANES - 2 + tm, ln]
                   + cw[1:2] * ext_ref[SUBLANES - 1:SUBLANES - 1 + tm, ln]
                   + cw[2:3] * ext_ref[SUBLANES:SUBLANES + tm, ln])
            carry_ref[:, cols] = ext_ref[tm:tm + SUBLANES, ln]
            return out

        a_up, g_up = (conv_half(cols, ln) for cols, ln in zip(halves(c), lanes))
        act_ref[:, c * fc:(c + 1) * fc] = (g_up * jax.nn.sigmoid(g_up) * a_up).astype(BF16)
        if (c + 1) % DOWN_GROUP == 0 or c + 1 == n_chunks:
            lo = (c // DOWN_GROUP) * DOWN_GROUP * fc
            acc_ref[...] += _dot(act_ref[:, lo:(c + 1) * fc], wdn_ref[lo:(c + 1) * fc, :])
    h_ref[...] = _rms(acc_ref[...], gfin_ref[...]) if final else acc_ref[...]

    @pl.when(j == pl.num_programs(1) - 1)
    def _():
        cst_ref[...] = carry_ref[...]


def _post(x, oa, ob, cbuf8, lw, tm, final_g, final):
    b, L, d = x.shape
    ff2 = lw["wup"].shape[1]
    fc = lw["fc"]
    n_chunks = ff2 // (2 * fc)
    row = lambda w: pl.BlockSpec((None, tm, w), lambda i, j: (i, j, 0))
    st = pl.BlockSpec((None, SUBLANES, ff2), lambda i, j: (i, 0, 0))
    weights = [lw["g_mix"], lw["wg"], lw["wabr"], lw["wbbr"], lw["wout"], lw["g_ffn"], lw["wup"],
               lw["conv_w"], lw["conv_b"], lw["wdn"], final_g.reshape(1, d)]
    return pl.pallas_call(
        functools.partial(_post_kernel, tm=tm, fc=fc, n_chunks=n_chunks, final=final),
        grid=(b, L // tm),
        in_specs=[row(d), row(oa.shape[-1]), row(ob.shape[-1])] + [_const_spec(w.shape) for w in weights] + [st],
        out_specs=[row(d), st],
        out_shape=[jax.ShapeDtypeStruct((b, L, d), F32), jax.ShapeDtypeStruct((b, SUBLANES, ff2), F32)],
        scratch_shapes=[pltpu.VMEM((tm, d), F32), pltpu.VMEM((tm + SUBLANES, 2 * fc), F32),
                        pltpu.VMEM((tm + SUBLANES, 2 * fc), F32), pltpu.VMEM((SUBLANES, ff2), F32),
                        pltpu.VMEM((tm, ff2 // 2), BF16)],
        compiler_params=pltpu.CompilerParams(dimension_semantics=("parallel", "arbitrary"),
                                             vmem_limit_bytes=VMEM_LIMIT),
    )(x, oa, ob, *weights, cbuf8)


def _layer_weights(l, norm_mix_g, w_in, b_q_norm_g, w_b_uq, b_kv_norm_g, w_b_ukv, w_a_br, w_b_br, w_out,
                   norm_ffn_g, w_up, conv_w, conv_b, w_down):
    d = w_in.shape[1]
    da = H_A * 2 * HD_A
    q_lora = w_b_uq.shape[1]
    kvl = w_b_ukv.shape[1]
    d_ff = w_down.shape[1]
    w = w_in[l]
    o_cq = 3 * da
    o_ckv = o_cq + q_lora
    o_kr = o_ckv + kvl
    o_g = o_kr + ROPE_B
    bf = lambda a: a.astype(BF16)
    wv = w[:, 2 * da:3 * da]
    wkr = jnp.zeros((d, LANES), F32).at[:, ROPE_LO:ROPE_LO + ROPE_B].set(w[:, o_kr:o_g])
    wuq = jnp.pad(w_b_uq[l], ((0, 0), (0, 0), (0, HB_PAD - NOPE_B - ROPE_B))).reshape(q_lora, H_B * HB_PAD)
    wuk = jnp.pad(w_b_ukv[l][:, :, :NOPE_B], ((0, 0), (0, 0), (0, HB_PAD - NOPE_B))).reshape(kvl, H_B * HB_PAD)
    wuv = w_b_ukv[l][:, :, NOPE_B:].reshape(kvl, H_B * V_B)
    fc = 256 if d_ff % 256 == 0 else d_ff
    return dict(
        g_mix=norm_mix_g[l].reshape(1, d), wq=bf(w[:, :da]), wq_t=bf(w[:, :da].T), wk=bf(w[:, da:2 * da]),
        wv=bf(wv),
        wcq=bf(w[:, o_cq:o_ckv]), wckv=bf(w[:, o_ckv:o_kr]), wkr=bf(wkr), wg=bf(w[:, o_g:]),
        g_q=b_q_norm_g[l].reshape(1, q_lora), wuq=bf(wuq), wuq_t=bf(wuq.T),
        g_kv=b_kv_norm_g[l].reshape(1, kvl), wuk=bf(wuk),
        wuv=bf(wuv), wuv_t=bf(wuv.T), wabr=bf(w_a_br[l]), wbbr=bf(w_b_br[l]), wout=bf(w_out[l]),
        g_ffn=norm_ffn_g[l].reshape(1, d), wup=bf(w_up[l]),
        conv_w=jnp.pad(conv_w[l], ((0, SUBLANES - CONV_W), (0, 0))),
        conv_b=conv_b[l].reshape(1, 2 * d_ff), wdn=bf(w_down[l]), fc=fc)


def _conv_state_rows(cst):
    return cst[:, SUBLANES - (CONV_W - 1):, :]


def _conv_state_block(buf):
    return jnp.pad(buf, ((0, 0), (SUBLANES - (CONV_W - 1), 0), (0, 0)))


def _rope_tables(pos):
    inv = ROPE_THETA ** (-jnp.arange(ROPE_HALF, dtype=F32) / ROPE_HALF)
    ang = jnp.asarray(pos).astype(F32)[:, None] * inv[None, :]
    c, s = jnp.cos(ang), jnp.sin(ang)
    z = jnp.zeros_like(c)
    n = pos.shape[0]
    tail = jnp.zeros((n, LANES - ROPE_LO - ROPE_B), F32)
    head0 = jnp.zeros((n, ROPE_LO), F32)
    cos = jnp.concatenate([jnp.ones((n, ROPE_LO), F32), c, c, tail], axis=1)
    sin_lo = jnp.concatenate([head0, -s, z, tail], axis=1)
    sin_hi = jnp.concatenate([head0, z, s, tail], axis=1)
    return cos, sin_lo, sin_hi, c.T, s.T


def _pad_rows(a, n, axis):
    pad = [(0, 0)] * a.ndim
    pad[axis] = (0, n - a.shape[axis])
    return jnp.pad(a, pad)


def _feature_major_values(v, heads, dv):
    b, rows, _ = v.shape
    vt = jnp.swapaxes(v, 1, 2).reshape(b, heads, dv, rows)
    extra = jnp.zeros((b, heads, BF16_ROWS, rows), BF16).at[:, :, 0, :].set(1.0)
    return jnp.concatenate([vt, extra], axis=2).reshape(b, heads * (dv + BF16_ROWS), rows)


def _row_tile(L, want):
    t = min(L, want)
    while L % t:
        t //= 2
    return t


def kernel(x_prompt, x_sample, cache_a_k, cache_a_v, cache_b_ckv, cache_b_krope, state_ffn_conv, meta_tokens,
           rel_bias_table, norm_mix_g, w_in, a_lambda, a_subln_g, b_q_norm_g, w_b_uq, b_kv_norm_g, w_b_ukv,
           w_a_br, w_b_br, w_out, norm_ffn_g, w_up, conv_w, conv_b, w_down, final_norm_g):
    bp, seq, d = x_prompt.shape
    bs, dec, _ = x_sample.shape
    depth = w_in.shape[0]
    past = cache_a_k.shape[2]
    da = H_A * 2 * HD_A
    ff2 = w_up.shape[2]
    assert seq % CHUNK == 0 and N_META % SUBLANES == 0 and dec % SUBLANES == 0

    meta_pos = np.arange(-N_META, 0)
    prompt_pos = np.arange(seq)
    sample_pos = np.arange(past, past + dec)
    tabs_m, tabs_p, tabs_s = _rope_tables(meta_pos), _rope_tables(prompt_pos), _rope_tables(sample_pos)

    tm_p = _row_tile(seq, 256)
    tq_p = sub_p = _row_tile(seq, 256)
    tk_p = MAX_SUB * sub_p if seq % (MAX_SUB * sub_p) == 0 else sub_p
    qpad = lambda a: _pad_rows(jnp.swapaxes(a, 1, 2), LANES, 2)
    mq_pos = np.concatenate([meta_pos, np.full(LANES - N_META, -1)])
    mk_valid = np.arange(LANES) < N_META
    sq_pos = np.concatenate([sample_pos, np.full(LANES - dec, sample_pos[-1])])
    lk_s = -(-(past + dec) // LANES) * LANES
    sk_pos = np.concatenate([np.arange(past + dec), np.zeros(lk_s - past - dec, np.int64)])
    sk_valid = np.arange(lk_s) < past + dec
    tm_c = _row_tile(past, 256)
    plan_m = _attention_plan(rel_bias_table, mq_pos, mq_pos, mk_valid, LANES, LANES, LANES, has_meta=False)
    plan_p = _attention_plan(rel_bias_table, prompt_pos, prompt_pos, np.ones(seq, bool), tq_p, tk_p, sub_p,
                             has_meta=True)
    plan_s = _attention_plan(rel_bias_table, sq_pos, sk_pos, sk_valid, LANES, lk_s, lk_s, has_meta=True)

    h_m = meta_tokens[None].astype(F32)
    h_p, h_s = x_prompt, x_sample
    outs = {k: [] for k in ("ak_s", "av_s", "ck_s", "kr_s", "cv_p", "cv_s")}
    meta_rows = {name: [] for name in CACHE_NAMES}
    cache_p = None
    for l in range(depth):
        lam_init = 0.8 - 0.6 * math.exp(-0.3 * l)
        lw = _layer_weights(l, norm_mix_g, w_in, b_q_norm_g, w_b_uq, b_kv_norm_g, w_b_ukv, w_a_br, w_b_br,
                            w_out, norm_ffn_g, w_up, conv_w, conv_b, w_down)
        attn = functools.partial(_attention, a_lambda=a_lambda[l], subln_g=a_subln_g[l], lam_init=lam_init)

        pm = _projection(h_m, lw, tabs_m, N_META, transposed=False)
        meta = dict(ka=pm["k16"][0], vat=_pad_rows(_feature_major_values(pm["v16"], H_A, DV_A)[0], LANES, 1),
                    kb=pm["kb"][0], vbt=_pad_rows(_feature_major_values(pm["vb"], H_B, V_B)[0], LANES, 1))
        qm = dict(qa1=qpad(pm["qa1"]), qa2=qpad(pm["qa2"]), qb=qpad(pm["qb"]),
                  ka=_pad_rows(meta["ka"], LANES, 0)[None], vat=meta["vat"][None],
                  kb=_pad_rows(meta["kb"], LANES, 0)[None], vbt=meta["vbt"][None])
        oa, ob = attn(qm, None, plan_m)
        post = functools.partial(_post, final_g=final_norm_g)
        h_m, cst_m = post(h_m, oa[:, :N_META], ob[:, :N_META], jnp.zeros((1, SUBLANES, ff2), F32), lw, N_META,
                          final=False)

        pp = _projection(h_p, lw, tabs_p, tm_p, transposed=True, cache=(depth, l, N_META, cache_p))
        cache_p = {name: pp[name] for name in CACHE_NAMES}
        qp = dict(qa1=pp["qa1"], qa2=pp["qa2"], qb=pp["qb"], ka=pp["k16"], vat=pp["v16"], kb=pp["kb"],
                  vbt=pp["vb"])
        oa, ob = attn(qp, meta, plan_p)
        last = l == depth - 1
        h_p, cst_p = post(h_p, oa, ob, jnp.broadcast_to(cst_m, (bp, SUBLANES, ff2)), lw, tm_p, final=last)

        ps = _projection(h_s, lw, tabs_s, dec, transposed=False)
        kr_cache = jnp.pad(cache_b_krope[l], ((0, 0), (0, 0), (ROPE_LO, LANES - ROPE_LO - ROPE_B)))
        kb_c, vbt_c = _kv_up_cache(cache_b_ckv[l], kr_cache, lw, tm_c)
        va_all = jnp.concatenate([cache_a_v[l].reshape(bs, past, da).astype(BF16), ps["v16"]], axis=1)
        vb_all = jnp.concatenate([jnp.swapaxes(vbt_c, 1, 2), ps["vb"]], axis=1)
        qs = dict(qa1=qpad(ps["qa1"]), qa2=qpad(ps["qa2"]), qb=qpad(ps["qb"]),
                  ka=_pad_rows(jnp.concatenate([cache_a_k[l].reshape(bs, past, da).astype(BF16), ps["k16"]],
                                               axis=1), lk_s, 1),
                  vat=_pad_rows(_feature_major_values(va_all, H_A, DV_A), lk_s, 2),
                  kb=_pad_rows(jnp.concatenate([kb_c, ps["kb"]], axis=1), lk_s, 1),
                  vbt=_pad_rows(_feature_major_values(vb_all, H_B, V_B), lk_s, 2))
        oa, ob = attn(qs, meta, plan_s)
        h_s, cst_s = post(h_s, oa[:, :dec], ob[:, :dec], _conv_state_block(state_ffn_conv[l]), lw, dec,
                          final=last)

        heads = lambda a: a.reshape(a.shape[:-1] + (H_A, 2 * HD_A))
        for name in CACHE_NAMES:
            meta_rows[name].append(pm[name])
        outs["cv_p"].append(_conv_state_rows(cst_p))
        outs["ak_s"].append(heads(ps["kf"]))
        outs["av_s"].append(heads(ps["vf"]))
        outs["ck_s"].append(ps["ckv"])
        outs["kr_s"].append(ps["kr"])
        outs["cv_s"].append(_conv_state_rows(cst_s))

    y_p, y_s = h_p, h_s
    st = lambda k: jnp.stack(outs[k])

    def with_meta_rows(name):
        rows = jnp.stack(meta_rows[name])
        rows = jnp.broadcast_to(rows, (depth, bp) + rows.shape[2:])
        return cache_p[name].at[:, :, :N_META].set(rows)

    return (y_p, y_s, heads(with_meta_rows("kf")), st("ak_s"), heads(with_meta_rows("vf")), st("av_s"),
            with_meta_rows("ckv"), st("ck_s"), with_meta_rows("kr"), st("kr_s"), st("cv_p"), st("cv_s"))
```
